```python
import math
import jax, jax.numpy as jnp
from jax import lax
import numpy as np

D_MODEL = 1024
BATCH = 8
SEQ = 2048
DEPTH = 1

ROPE_THETA = 500000.0
Q_BLOCK = 128
NORM_EPS = 1e-6
SUBLN_EPS = 1e-5

MLA_HEADS = 4
MLA_NOPE = 128
MLA_ROPE = 64
MLA_V = 128
KV_RANK = 128
MLA_WIDTH = MLA_HEADS * MLA_V

DIFF_HEADS = 4
DIFF_QK = 64
DIFF_V = 2 * DIFF_QK
DIFF_ROPE = DIFF_QK // 4
DIFF_WIDTH = DIFF_HEADS * DIFF_V

MIX_WIDTH = MLA_WIDTH + DIFF_WIDTH

PROJ_SIZES = (
    MLA_HEADS * MLA_NOPE,
    MLA_HEADS * MLA_ROPE,
    KV_RANK,
    MLA_ROPE,
    MLA_WIDTH,
    DIFF_HEADS * 2 * DIFF_QK,
    DIFF_HEADS * 2 * DIFF_QK,
    DIFF_WIDTH,
    DIFF_WIDTH,
)
PROJ_OUT = sum(PROJ_SIZES)
PROJ_SPLITS = tuple(int(s) for s in np.cumsum(PROJ_SIZES)[:-1])

kernel_name = "hybrid_mla_diffattn_parallel_heads"


def rmsnorm(x, g, eps=NORM_EPS):
    xf = x.astype(jnp.float32)
    y = xf * lax.rsqrt(jnp.mean(xf * xf, axis=-1, keepdims=True) + eps)
    return (y * g.astype(jnp.float32)).astype(x.dtype)


def rope_tables(seq, dim):
    inv_freq = ROPE_THETA ** (-jnp.arange(0, dim, 2, dtype=jnp.float32) / dim)
    ang = jnp.arange(seq, dtype=jnp.float32)[:, None] * inv_freq[None, :]
    return jnp.cos(ang), jnp.sin(ang)


def apply_rope(x, cos, sin):
    half = cos.shape[-1]
    shp = (1, x.shape[1]) + (1,) * (x.ndim - 3) + (half,)
    c = cos.reshape(shp).astype(x.dtype)
    s = sin.reshape(shp).astype(x.dtype)
    x1 = x[..., :half]
    x2 = x[..., half:2 * half]
    rot = jnp.concatenate([x1 * c - x2 * s, x2 * c + x1 * s], axis=-1)
    return jnp.concatenate([rot, x[..., 2 * half:]], axis=-1)


def causal_attention(q, k, v, scale):
    B, S, H, Dk = q.shape
    Dv = v.shape[-1]
    nb = S // Q_BLOCK
    qb = q.reshape(B, nb, Q_BLOCK, H, Dk).transpose(1, 0, 2, 3, 4)
    kpos = jnp.arange(S)

    def one_block(args):
        qi, i = args
        s = jnp.einsum('bqhd,bkhd->bhqk', qi, k).astype(jnp.float32) * scale
        qpos = i * Q_BLOCK + jnp.arange(Q_BLOCK)
        mask = kpos[None, :] <= qpos[:, None]
        s = jnp.where(mask[None, None], s, -jnp.inf)
        p = jax.nn.softmax(s, axis=-1).astype(v.dtype)
        return jnp.einsum('bhqk,bkhd->bqhd', p, v)

    out = lax.map(one_block, (qb, jnp.arange(nb)))
    return out.transpose(1, 0, 2, 3, 4).reshape(B, S, H, Dv)


def setup_inputs(seed: int = 0) -> dict:
    key = jax.random.key(seed)
    ks = jax.random.split(key, 14)
    nrm = jax.random.normal
    f32 = jnp.float32
    return {
        "x": nrm(ks[0], (BATCH, SEQ, D_MODEL), f32),
        "ln_pre_g": 1.0 + 0.02 * nrm(ks[1], (DEPTH, D_MODEL), f32),
        "w_in": nrm(ks[2], (DEPTH, D_MODEL, PROJ_OUT), f32) * D_MODEL ** -0.5,
        "kv_norm_g": 1.0 + 0.02 * nrm(ks[3], (DEPTH, KV_RANK), f32),
        "w_uk": nrm(ks[4], (DEPTH, KV_RANK, MLA_HEADS * MLA_NOPE), f32) * KV_RANK ** -0.5,
        "w_uv": nrm(ks[5], (DEPTH, KV_RANK, MLA_HEADS * MLA_V), f32) * KV_RANK ** -0.5,
        "lambda_q1": 0.1 * nrm(ks[6], (DEPTH, DIFF_QK), f32),
        "lambda_k1": 0.1 * nrm(ks[7], (DEPTH, DIFF_QK), f32),
        "lambda_q2": 0.1 * nrm(ks[8], (DEPTH, DIFF_QK), f32),
        "lambda_k2": 0.1 * nrm(ks[9], (DEPTH, DIFF_QK), f32),
        "subln_g": 1.0 + 0.02 * nrm(ks[10], (DEPTH, DIFF_V), f32),
        "w_out": nrm(ks[11], (DEPTH, MIX_WIDTH, D_MODEL), f32) * MIX_WIDTH ** -0.5,
        "ln_post_g": 1.0 + 0.02 * nrm(ks[12], (DEPTH, D_MODEL), f32),
    }


def reference(x, ln_pre_g, w_in, kv_norm_g, w_uk, w_uv, lambda_q1, lambda_k1,
              lambda_q2, lambda_k2, subln_g, w_out, ln_post_g):
    B, S, _ = x.shape
    cos_a, sin_a = rope_tables(S, MLA_ROPE)
    cos_b, sin_b = rope_tables(S, DIFF_ROPE)
    mla_scale = 1.0 / math.sqrt(MLA_NOPE + MLA_ROPE)
    diff_scale = 1.0 / math.sqrt(DIFF_QK)

    for l in range(DEPTH):
        lambda_init = 0.8 - 0.6 * math.exp(-0.3 * l)
        h = rmsnorm(x, ln_pre_g[l])
        proj = h @ w_in[l]
        (q_nope, q_rope, c_kv, k_rope, g_a,
         dq, dk, dv, g_b) = jnp.split(proj, PROJ_SPLITS, axis=-1)

        q_nope = q_nope.reshape(B, S, MLA_HEADS, MLA_NOPE)
        q_rope = apply_rope(q_rope.reshape(B, S, MLA_HEADS, MLA_ROPE), cos_a, sin_a)
        c_kv = rmsnorm(c_kv, kv_norm_g[l])
        k_nope = (c_kv @ w_uk[l]).reshape(B, S, MLA_HEADS, MLA_NOPE)
        v_a = (c_kv @ w_uv[l]).reshape(B, S, MLA_HEADS, MLA_V)
        k_rope = apply_rope(k_rope[:, :, None, :], cos_a, sin_a)
        k_rope = jnp.broadcast_to(k_rope, (B, S, MLA_HEADS, MLA_ROPE))
        q_a = jnp.concatenate([q_nope, q_rope], axis=-1)
        k_a = jnp.concatenate([k_nope, k_rope], axis=-1)
        o_a = causal_attention(q_a, k_a, v_a, mla_scale).reshape(B, S, MLA_WIDTH)
        o_a = o_a * jax.nn.silu(g_a)

        dq = apply_rope(dq.reshape(B, S, DIFF_HEADS, 2, DIFF_QK), cos_b, sin_b)
        dk = apply_rope(dk.reshape(B, S, DIFF_HEADS, 2, DIFF_QK), cos_b, sin_b)
        dv = dv.reshape(B, S, DIFF_HEADS, DIFF_V)
        lam = (jnp.exp(jnp.sum(lambda_q1[l].astype(jnp.float32) * lambda_k1[l].astype(jnp.float32)))
               - jnp.exp(jnp.sum(lambda_q2[l].astype(jnp.float32) * lambda_k2[l].astype(jnp.float32)))
               + lambda_init)
        a1 = causal_attention(dq[..., 0, :], dk[..., 0, :], dv, diff_scale)
        a2 = causal_attention(dq[..., 1, :], dk[..., 1, :], dv, diff_scale)
        o_b = a1 - lam.astype(a1.dtype) * a2
        o_b = rmsnorm(o_b, subln_g[l], SUBLN_EPS) * (1.0 - lambda_init)
        o_b = o_b.reshape(B, S, DIFF_WIDTH) * jax.nn.silu(g_b)

        mixed = jnp.concatenate([o_a, o_b], axis=-1) @ w_out[l]
        x = x + rmsnorm(mixed, ln_post_g[l])
    return x
```

```python
import functools
import math

import jax
import jax.numpy as jnp
import numpy as np
from jax import lax
from jax.experimental import pallas as pl
from jax.experimental.pallas import tpu as pltpu

ROPE_THETA = 500000.0
NORM_EPS = 1e-6
SUBLN_EPS = 1e-5

MLA_HEADS = 4
MLA_NOPE = 128
MLA_ROPE = 64
MLA_V = 128
KV_RANK = 128
DIFF_HEADS = 4
DIFF_QK = 64
DIFF_V = 2 * DIFF_QK
DIFF_ROPE = DIFF_QK // 4

LANES = 128
BF16_ROWS = 16
MLA_QK_PAD = 2 * LANES
LOG2E = math.log2(math.e)
MLA_QSCALE = LOG2E / math.sqrt(MLA_NOPE + MLA_ROPE)
DIFF_QSCALE = LOG2E / math.sqrt(DIFF_QK)

PROJ_TILE = 512
Q_TILE = 256
KV_TILE = 256
ACC_ROWS = MLA_V + BF16_ROWS
VMEM_LIMIT_BYTES = 56 * 1024 * 1024

BF16 = jnp.bfloat16
F32 = jnp.float32
_NT = (((1,), (1,)), ((), ()))


def _w_in_layout():
    q_nope = 0
    q_rope = q_nope + MLA_HEADS * MLA_NOPE
    c_kv = q_rope + MLA_HEADS * MLA_ROPE
    k_rope = c_kv + KV_RANK
    g_a = k_rope + MLA_ROPE
    dq = g_a + MLA_HEADS * MLA_V
    dk = dq + DIFF_HEADS * 2 * DIFF_QK
    dv = dk + DIFF_HEADS * 2 * DIFF_QK
    g_b = dv + DIFF_HEADS * DIFF_V
    end = g_b + DIFF_HEADS * DIFF_V
    half = MLA_ROPE // 2
    rh = DIFF_ROPE // 2

    def diff_perm(base):
        cols = []
        for h in range(DIFF_HEADS):
            m0 = base + h * 2 * DIFF_QK
            m1 = m0 + DIFF_QK
            cols += list(range(m0, m0 + rh)) + list(range(m1, m1 + rh))
            cols += list(range(m0 + 2 * rh, m0 + DIFF_QK))
            cols += list(range(m0 + rh, m0 + 2 * rh)) + list(range(m1 + rh, m1 + 2 * rh))
            cols += list(range(m1 + 2 * rh, m1 + DIFF_QK))
        return cols

    x1 = list(range(k_rope, k_rope + half))
    x2 = list(range(k_rope + half, k_rope + 2 * half))
    tok = list(range(c_kv, k_rope)) + x1 + x1 + x2 + x2
    tok += list(range(g_a, dq)) + diff_perm(dk) + list(range(g_b, end))
    feat = list(range(q_nope, c_kv)) + diff_perm(dq) + list(range(dv, g_b))
    return np.asarray(tok, np.int32), np.asarray(feat, np.int32)


_TOK_COLS, _FEAT_COLS = _w_in_layout()
_T_CKV, _T_KR, _T_GA, _T_DK, _T_GB = 0, 128, 256, 768, 1280
_F_QN, _F_QR, _F_DQ, _F_DV = 0, 512, 768, 1280


def _rope_tables(seq):
    def tab(dim):
        inv_freq = ROPE_THETA ** (-jnp.arange(0, dim, 2, dtype=F32) / dim)
        ang = jnp.arange(seq, dtype=F32)[:, None] * inv_freq[None, :]
        return jnp.cos(ang), jnp.sin(ang)
    ca, sa = tab(MLA_ROPE)
    cb, sb = tab(DIFF_ROPE)
    one = jnp.ones((seq, LANES // 2 - 2 * cb.shape[1]), F32)
    zero = jnp.zeros_like(one)
    tok = (jnp.concatenate([ca, ca, ca, ca], axis=1),
           jnp.concatenate([-sa, -sa, sa, sa], axis=1),
           jnp.concatenate([cb, cb, one, cb, cb, one], axis=1),
           jnp.concatenate([-sb, -sb, zero, sb, sb, zero], axis=1))
    feat = (ca.T, sa.T, jnp.concatenate([cb, cb], axis=1).T, jnp.concatenate([sb, sb], axis=1).T)
    return tok + feat


def _silu(g):
    return g / (1.0 + jnp.exp(-g))


def _rope_lanes(v, cos, sin):
    return v * cos + pltpu.roll(v, LANES // 2, axis=1) * sin


def _proj_kernel(x_ref, g_ref, wtok_ref, wfeat_ref, kvg_ref, wuk_ref, wuvt_ref,
                 ca_ref, sa_ref, cb_ref, sb_ref, cat_ref, sat_ref, cbt_ref, sbt_ref,
                 qat_ref, dq1t_ref, dq2t_ref, vat_ref, dvt_ref, ka_ref, dk_ref, ga_ref, gb_ref):
    x = x_ref[...]
    ms = jnp.mean(x * x, axis=-1, keepdims=True)
    h = (x * lax.rsqrt(ms + NORM_EPS) * g_ref[...]).astype(BF16)

    feat = lax.dot_general(wfeat_ref[...], h, _NT, preferred_element_type=F32)
    cat, sat = cat_ref[...], sat_ref[...]
    half = MLA_ROPE // 2
    zpad = jnp.zeros((half, PROJ_TILE), BF16)
    for hd in range(MLA_HEADS):
        base = hd * MLA_QK_PAD
        qn = feat[_F_QN + hd * MLA_NOPE:_F_QN + (hd + 1) * MLA_NOPE]
        qat_ref[base:base + MLA_NOPE, :] = (qn * MLA_QSCALE).astype(BF16)
        x1 = feat[_F_QR + hd * MLA_ROPE:_F_QR + hd * MLA_ROPE + half]
        x2 = feat[_F_QR + hd * MLA_ROPE + half:_F_QR + (hd + 1) * MLA_ROPE]
        r1 = (x1 * cat - x2 * sat) * MLA_QSCALE
        r2 = (x2 * cat + x1 * sat) * MLA_QSCALE
        rope_rows = jnp.concatenate([r1.astype(BF16), zpad, r2.astype(BF16), zpad], axis=0)
        qat_ref[base + MLA_NOPE:base + MLA_QK_PAD, :] = rope_rows

    cbt, sbt = cbt_ref[...], sbt_ref[...]
    rh = DIFF_ROPE // 2
    frow = lax.broadcasted_iota(jnp.int32, (LANES, PROJ_TILE), 0)
    first_map = ((frow % (LANES // 2)) < rh) | ((frow >= 2 * rh) & (frow < LANES // 2))
    for hd in range(DIFF_HEADS):
        d = feat[_F_DQ + hd * LANES:_F_DQ + (hd + 1) * LANES]
        x1, x2 = d[:2 * rh], d[LANES // 2:LANES // 2 + 2 * rh]
        q = jnp.concatenate([x1 * cbt - x2 * sbt, d[2 * rh:LANES // 2],
                             x2 * cbt + x1 * sbt, d[LANES // 2 + 2 * rh:]], axis=0) * DIFF_QSCALE
        sl = slice(hd * LANES, (hd + 1) * LANES)
        dq1t_ref[sl, :] = jnp.where(first_map, q, 0.0).astype(BF16)
        dq2t_ref[sl, :] = jnp.where(first_map, 0.0, q).astype(BF16)
    for c in range(PROJ_TILE // KV_TILE):
        dvt_ref[c] = feat[_F_DV:, c * KV_TILE:(c + 1) * KV_TILE].astype(BF16)

    tok = jnp.dot(h, wtok_ref[...], preferred_element_type=F32)
    ca, sa, cb, sb = ca_ref[...], sa_ref[...], cb_ref[...], sb_ref[...]
    c = tok[:, _T_CKV:_T_CKV + KV_RANK]
    c = c * lax.rsqrt(jnp.mean(c * c, axis=-1, keepdims=True) + NORM_EPS) * kvg_ref[...]
    c = c.astype(BF16)
    kn = jnp.dot(c, wuk_ref[...], preferred_element_type=F32)
    vt = lax.dot_general(wuvt_ref[...], c, _NT, preferred_element_type=F32)
    for ci in range(PROJ_TILE // KV_TILE):
        vat_ref[ci] = vt[:, ci * KV_TILE:(ci + 1) * KV_TILE].astype(BF16)
    kr = _rope_lanes(tok[:, _T_KR:_T_KR + LANES], ca, sa).astype(BF16)
    for hd in range(MLA_HEADS):
        base = hd * MLA_QK_PAD
        ka_ref[:, base:base + LANES] = kn[:, hd * LANES:(hd + 1) * LANES].astype(BF16)
        ka_ref[:, base + LANES:base + 2 * LANES] = kr
    ga_ref[...] = _silu(tok[:, _T_GA:_T_GA + 512]).astype(BF16)
    for hd in range(DIFF_HEADS):
        sl = slice(_T_DK + hd * LANES, _T_DK + (hd + 1) * LANES)
        dk_ref[:, hd * LANES:(hd + 1) * LANES] = _rope_lanes(tok[:, sl], cb, sb).astype(BF16)
    gb_ref[...] = _silu(tok[:, _T_GB:_T_GB + 512]).astype(BF16)


def _online_softmax_step(k, qt, vt, m_ref, acc_ref, idx, mask):
    s_t = jnp.dot(k, qt, preferred_element_type=F32)
    if mask is not None:
        s_t = jnp.where(mask, s_t, -jnp.inf)
    m_old = m_ref[idx]
    m_new = jnp.maximum(m_old, jnp.max(s_t, axis=0, keepdims=True))
    alpha = jnp.exp2(m_old - m_new)
    p_t = jnp.exp2(s_t - m_new).astype(BF16)
    acc_ref[idx] = alpha * acc_ref[idx] + jnp.dot(vt, p_t, preferred_element_type=F32)
    m_ref[idx] = m_new


def _attn_kernel(qat_ref, dq1t_ref, dq2t_ref, ga_ref, gb_ref, x_ref,
                 ka_ref, vat_ref, dk_ref, dvt_ref, wout_ref,
                 lq1_ref, lk1_ref, lq2_ref, lk2_ref, subg_ref, postg_ref,
                 out_ref, ma_ref, acca_ref, mb_ref, accb_ref, *, lambda_init):
    qi = pl.program_id(1)
    ma_ref[...] = jnp.full(ma_ref.shape, -jnp.inf, F32)
    mb_ref[...] = jnp.full(mb_ref.shape, -jnp.inf, F32)
    acca_ref[...] = jnp.zeros(acca_ref.shape, F32)
    accb_ref[...] = jnp.zeros(accb_ref.shape, F32)

    ones = jnp.ones((BF16_ROWS, KV_TILE), BF16)
    key = lax.broadcasted_iota(jnp.int32, (KV_TILE, Q_TILE), 0)
    qry = lax.broadcasted_iota(jnp.int32, (KV_TILE, Q_TILE), 1)
    causal = key <= qry
    causal2 = jnp.concatenate([causal, causal], axis=1)

    def kv_step(j, diagonal):
        rows = pl.ds(pl.multiple_of(j * KV_TILE, KV_TILE), KV_TILE)
        for h in range(MLA_HEADS):
            cs = slice(h * MLA_QK_PAD, (h + 1) * MLA_QK_PAD)
            vt = jnp.concatenate([vat_ref[j, h * MLA_V:(h + 1) * MLA_V, :], ones], axis=0)
            _online_softmax_step(ka_ref[rows, cs], qat_ref[cs, :], vt, ma_ref, acca_ref, h,
                                 causal if diagonal else None)
        for h in range(DIFF_HEADS):
            cs = slice(h * LANES, (h + 1) * LANES)
            qt = jnp.concatenate([dq1t_ref[cs, :], dq2t_ref[cs, :]], axis=1)
            vt = jnp.concatenate([dvt_ref[j, cs, :], ones], axis=0)
            _online_softmax_step(dk_ref[rows, cs], qt, vt, mb_ref, accb_ref, h,
                                 causal2 if diagonal else None)

    def body(j, carry):
        kv_step(j, False)
        return carry

    lax.fori_loop(0, qi, body, 0)
    kv_step(qi, True)

    oa = []
    for h in range(MLA_HEADS):
        acc = acca_ref[h]
        o_t = acc[:MLA_V] * (1.0 / acc[MLA_V:MLA_V + 1])
        oa.append(o_t.T)
    oa = jnp.concatenate(oa, axis=1) * ga_ref[...].astype(F32)

    lam = (jnp.exp(jnp.sum(lq1_ref[...] * lk1_ref[...], axis=1, keepdims=True))
           - jnp.exp(jnp.sum(lq2_ref[...] * lk2_ref[...], axis=1, keepdims=True))
           + lambda_init)
    ob = []
    for h in range(DIFF_HEADS):
        acc = accb_ref[h]
        inv_l = 1.0 / acc[DIFF_V:DIFF_V + 1]
        a = acc[:DIFF_V] * inv_l
        o_t = a[:, :Q_TILE] - lam * a[:, Q_TILE:]
        o_t = o_t * lax.rsqrt(jnp.mean(o_t * o_t, axis=0, keepdims=True) + SUBLN_EPS)
        ob.append(o_t.T * subg_ref[...] * (1.0 - lambda_init))
    ob = jnp.concatenate(ob, axis=1) * gb_ref[...].astype(F32)

    mixed_in = jnp.concatenate([oa, ob], axis=1).astype(BF16)
    mixed = jnp.dot(mixed_in, wout_ref[...], preferred_element_type=F32)
    y = mixed * lax.rsqrt(jnp.mean(mixed * mixed, axis=-1, keepdims=True) + NORM_EPS)
    out_ref[...] = x_ref[...] + y * postg_ref[...]


def _layer(x3, ln_pre_g, w_in, kv_norm_g, w_uk, w_uv, lq1, lk1, lq2, lk2,
           subln_g, w_out, ln_post_g, tables, lambda_init):
    batch, seq, d_model = x3.shape
    w_tok = jnp.take(w_in, _TOK_COLS, axis=1).astype(BF16)
    w_feat = jnp.take(w_in, _FEAT_COLS, axis=1).T.astype(BF16)
    n_tok, n_feat = w_tok.shape[1], w_feat.shape[0]
    g2 = ln_pre_g.reshape(1, d_model)
    kvg2 = kv_norm_g.reshape(1, KV_RANK)
    wuk = w_uk.astype(BF16)
    wuvt = w_uv.T.astype(BF16)

    seq_tiles = seq // PROJ_TILE
    kv_per_tile = PROJ_TILE // KV_TILE
    n_kv = seq // KV_TILE
    tok_blk = lambda w: pl.BlockSpec((None, PROJ_TILE, w), lambda b, t: (b, t, 0))
    feat_blk = lambda r: pl.BlockSpec((None, r, PROJ_TILE), lambda b, t: (b, 0, t))
    vt_blk = pl.BlockSpec((None, kv_per_tile, 512, KV_TILE), lambda b, t: (b, t, 0, 0))
    full = lambda a: pl.BlockSpec(a.shape, lambda b, t: (0,) * a.ndim)
    tab_tok = pl.BlockSpec((PROJ_TILE, LANES), lambda b, t: (t, 0))
    tab_feat = lambda r: pl.BlockSpec((r, PROJ_TILE), lambda b, t: (0, t))
    ca, sa, cb, sb, cat, sat, cbt, sbt = tables
    sds = jax.ShapeDtypeStruct
    qat, dq1t, dq2t, vat, dvt, ka, dk, ga, gb = pl.pallas_call(
        _proj_kernel,
        grid=(batch, seq_tiles),
        in_specs=[tok_blk(d_model), full(g2), full(w_tok), full(w_feat), full(kvg2),
                  full(wuk), full(wuvt), tab_tok, tab_tok, tab_tok, tab_tok,
                  tab_feat(cat.shape[0]), tab_feat(sat.shape[0]),
                  tab_feat(cbt.shape[0]), tab_feat(sbt.shape[0])],
        out_specs=[feat_blk(1024), feat_blk(512), feat_blk(512), vt_blk, vt_blk,
                   tok_blk(1024), tok_blk(512), tok_blk(512), tok_blk(512)],
        out_shape=[sds((batch, 1024, seq), BF16), sds((batch, 512, seq), BF16),
                   sds((batch, 512, seq), BF16),
                   sds((batch, n_kv, 512, KV_TILE), BF16), sds((batch, n_kv, 512, KV_TILE), BF16),
                   sds((batch, seq, 1024), BF16), sds((batch, seq, 512), BF16),
                   sds((batch, seq, 512), BF16), sds((batch, seq, 512), BF16)],
        compiler_params=pltpu.CompilerParams(
            dimension_semantics=("arbitrary", "arbitrary"), vmem_limit_bytes=VMEM_LIMIT_BYTES),
        name="proj",
    )(x3, g2, w_tok, w_feat, kvg2, wuk, wuvt, ca, sa, cb, sb, cat, sat, cbt, sbt)

    q_tok = lambda w: pl.BlockSpec((None, Q_TILE, w), lambda b, i: (b, i, 0))
    q_feat = lambda r: pl.BlockSpec((None, r, Q_TILE), lambda b, i: (b, 0, i))
    kv_tok = lambda w: pl.BlockSpec((None, seq, w), lambda b, i: (b, 0, 0))
    kv_feat = pl.BlockSpec((None, n_kv, 512, KV_TILE), lambda b, i: (b, 0, 0, 0))
    fullb = lambda a: pl.BlockSpec(a.shape, lambda b, i: (0,) * a.ndim)
    wout = w_out.astype(BF16)
    lvecs = [v.reshape(1, DIFF_QK) for v in (lq1, lk1, lq2, lk2)]
    subg2 = subln_g.reshape(1, DIFF_V)
    postg2 = ln_post_g.reshape(1, d_model)
    return pl.pallas_call(
        functools.partial(_attn_kernel, lambda_init=lambda_init),
        grid=(batch, seq // Q_TILE),
        in_specs=[q_feat(1024), q_feat(512), q_feat(512), q_tok(512), q_tok(512), q_tok(d_model),
                  kv_tok(1024), kv_feat, kv_tok(512), kv_feat, fullb(wout)]
                 + [fullb(v) for v in lvecs] + [fullb(subg2), fullb(postg2)],
        out_specs=q_tok(d_model),
        out_shape=sds((batch, seq, d_model), F32),
        scratch_shapes=[
            pltpu.VMEM((MLA_HEADS, 1, Q_TILE), F32),
            pltpu.VMEM((MLA_HEADS, ACC_ROWS, Q_TILE), F32),
            pltpu.VMEM((DIFF_HEADS, 1, 2 * Q_TILE), F32),
            pltpu.VMEM((DIFF_HEADS, ACC_ROWS, 2 * Q_TILE), F32),
        ],
        compiler_params=pltpu.CompilerParams(
            dimension_semantics=("arbitrary", "arbitrary"), vmem_limit_bytes=VMEM_LIMIT_BYTES),
        name="attn",
    )(qat, dq1t, dq2t, ga, gb, x3, ka, vat, dk, dvt, wout, *lvecs, subg2, postg2)


def kernel(x, ln_pre_g, w_in, kv_norm_g, w_uk, w_uv, lambda_q1, lambda_k1,
           lambda_q2, lambda_k2, subln_g, w_out, ln_post_g):
    batch, seq, d_model = x.shape
    depth = w_in.shape[0]
    assert seq % PROJ_TILE == 0 and PROJ_TILE % KV_TILE == 0 and Q_TILE == KV_TILE
    assert w_in.shape[2] == _TOK_COLS.size + _FEAT_COLS.size - MLA_ROPE and d_model == 1024
    tables = _rope_tables(seq)
    for l in range(depth):
        lambda_init = 0.8 - 0.6 * math.exp(-0.3 * l)
        x = _layer(x, ln_pre_g[l], w_in[l], kv_norm_g[l], w_uk[l], w_uv[l],
                   lambda_q1[l], lambda_k1[l], lambda_q2[l], lambda_k2[l], subln_g[l],
                   w_out[l], ln_post_g[l], tables, lambda_init)
    return x
```

```python
import functools
import math

import jax
import jax.numpy as jnp
import numpy as np
from jax import lax
from jax.experimental import pallas as pl
from jax.experimental.pallas import tpu as pltpu

ROPE_THETA = 500000.0
NORM_EPS = 1e-6
SUBLN_EPS = 1e-5

MLA_HEADS = 4
MLA_NOPE = 128
MLA_ROPE = 64
MLA_V = 128
KV_RANK = 128
DIFF_HEADS = 4
DIFF_QK = 64
DIFF_V = 2 * DIFF_QK
DIFF_ROPE = DIFF_QK // 4

LANES = 128
BF16_ROWS = 16
MLA_QK_PAD = 2 * LANES
LOG2E = math.log2(math.e)
MLA_QSCALE = LOG2E / math.sqrt(MLA_NOPE + MLA_ROPE)
DIFF_QSCALE = LOG2E / math.sqrt(DIFF_QK)

PROJ_TILE = 512
Q_TILE = 256
KV_TILE = 256
ACC_ROWS = MLA_V + BF16_ROWS
SCORE_LOOKAHEAD = 5
VMEM_LIMIT_BYTES = 56 * 1024 * 1024

BF16 = jnp.bfloat16
F32 = jnp.float32
_NT = (((1,), (1,)), ((), ()))


def _w_in_layout():
    q_nope = 0
    q_rope = q_nope + MLA_HEADS * MLA_NOPE
    c_kv = q_rope + MLA_HEADS * MLA_ROPE
    k_rope = c_kv + KV_RANK
    g_a = k_rope + MLA_ROPE
    dq = g_a + MLA_HEADS * MLA_V
    dk = dq + DIFF_HEADS * 2 * DIFF_QK
    dv = dk + DIFF_HEADS * 2 * DIFF_QK
    g_b = dv + DIFF_HEADS * DIFF_V
    end = g_b + DIFF_HEADS * DIFF_V
    half = MLA_ROPE // 2
    rh = DIFF_ROPE // 2

    def diff_perm(base):
        cols = []
        for h in range(DIFF_HEADS):
            m0 = base + h * 2 * DIFF_QK
            m1 = m0 + DIFF_QK
            cols += list(range(m0, m0 + rh)) + list(range(m1, m1 + rh))
            cols += list(range(m0 + 2 * rh, m0 + DIFF_QK))
            cols += list(range(m0 + rh, m0 + 2 * rh)) + list(range(m1 + rh, m1 + 2 * rh))
            cols += list(range(m1 + 2 * rh, m1 + DIFF_QK))
        return cols

    x1 = list(range(k_rope, k_rope + half))
    x2 = list(range(k_rope + half, k_rope + 2 * half))
    tok = list(range(c_kv, k_rope)) + x1 + x1 + x2 + x2
    tok += list(range(g_a, dq)) + diff_perm(dk) + list(range(g_b, end))
    feat = list(range(q_nope, c_kv)) + diff_perm(dq) + list(range(dv, g_b))
    return np.asarray(tok, np.int32), np.asarray(feat, np.int32)


_TOK_COLS, _FEAT_COLS = _w_in_layout()
_T_CKV, _T_KR, _T_GA, _T_DK, _T_GB = 0, 128, 256, 768, 1280
_F_QN, _F_QR, _F_DQ, _F_DV = 0, 512, 768, 1280


def _rope_tables(seq):
    def tab(dim):
        inv_freq = ROPE_THETA ** (-jnp.arange(0, dim, 2, dtype=F32) / dim)
        ang = jnp.arange(seq, dtype=F32)[:, None] * inv_freq[None, :]
        return jnp.cos(ang), jnp.sin(ang)
    ca, sa = tab(MLA_ROPE)
    cb, sb = tab(DIFF_ROPE)
    one = jnp.ones((seq, LANES // 2 - 2 * cb.shape[1]), F32)
    zero = jnp.zeros_like(one)
    tok = (jnp.concatenate([ca, ca, ca, ca], axis=1),
           jnp.concatenate([-sa, -sa, sa, sa], axis=1),
           jnp.concatenate([cb, cb, one, cb, cb, one], axis=1),
           jnp.concatenate([-sb, -sb, zero, sb, sb, zero], axis=1))
    feat = (ca.T, sa.T, jnp.concatenate([cb, cb], axis=1).T, jnp.concatenate([sb, sb], axis=1).T)
    return tok + feat


def _silu(g):
    return g / (1.0 + jnp.exp(-g))


def _rope_lanes(v, cos, sin):
    return v * cos + pltpu.roll(v, LANES // 2, axis=1) * sin


def _proj_kernel(x_ref, g_ref, wtok_ref, wfeat_ref, kvg_ref, wuk_ref, wuvt_ref,
                 ca_ref, sa_ref, cb_ref, sb_ref, cat_ref, sat_ref, cbt_ref, sbt_ref,
                 qat_ref, dq1t_ref, dq2t_ref, vat_ref, dvt_ref, ka_ref, dk_ref, ga_ref, gb_ref):
    x = x_ref[...]
    ms = jnp.mean(x * x, axis=-1, keepdims=True)
    h = (x * lax.rsqrt(ms + NORM_EPS) * g_ref[...]).astype(BF16)

    feat = lax.dot_general(wfeat_ref[...], h, _NT, preferred_element_type=F32)
    cat, sat = cat_ref[...], sat_ref[...]
    half = MLA_ROPE // 2
    zpad = jnp.zeros((half, PROJ_TILE), BF16)
    for hd in range(MLA_HEADS):
        base = hd * MLA_QK_PAD
        qn = feat[_F_QN + hd * MLA_NOPE:_F_QN + (hd + 1) * MLA_NOPE]
        qat_ref[base:base + MLA_NOPE, :] = (qn * MLA_QSCALE).astype(BF16)
        x1 = feat[_F_QR + hd * MLA_ROPE:_F_QR + hd * MLA_ROPE + half]
        x2 = feat[_F_QR + hd * MLA_ROPE + half:_F_QR + (hd + 1) * MLA_ROPE]
        r1 = (x1 * cat - x2 * sat) * MLA_QSCALE
        r2 = (x2 * cat + x1 * sat) * MLA_QSCALE
        rope_rows = jnp.concatenate([r1.astype(BF16), zpad, r2.astype(BF16), zpad], axis=0)
        qat_ref[base + MLA_NOPE:base + MLA_QK_PAD, :] = rope_rows

    cbt, sbt = cbt_ref[...], sbt_ref[...]
    rh = DIFF_ROPE // 2
    frow = lax.broadcasted_iota(jnp.int32, (LANES, PROJ_TILE), 0)
    first_map = ((frow % (LANES // 2)) < rh) | ((frow >= 2 * rh) & (frow < LANES // 2))
    for hd in range(DIFF_HEADS):
        d = feat[_F_DQ + hd * LANES:_F_DQ + (hd + 1) * LANES]
        x1, x2 = d[:2 * rh], d[LANES // 2:LANES // 2 + 2 * rh]
        q = jnp.concatenate([x1 * cbt - x2 * sbt, d[2 * rh:LANES // 2],
                             x2 * cbt + x1 * sbt, d[LANES // 2 + 2 * rh:]], axis=0) * DIFF_QSCALE
        sl = slice(hd * LANES, (hd + 1) * LANES)
        dq1t_ref[sl, :] = jnp.where(first_map, q, 0.0).astype(BF16)
        dq2t_ref[sl, :] = jnp.where(first_map, 0.0, q).astype(BF16)
    for c in range(PROJ_TILE // KV_TILE):
        dvt_ref[c] = feat[_F_DV:, c * KV_TILE:(c + 1) * KV_TILE].astype(BF16)

    tok = jnp.dot(h, wtok_ref[...], preferred_element_type=F32)
    ca, sa, cb, sb = ca_ref[...], sa_ref[...], cb_ref[...], sb_ref[...]
    c = tok[:, _T_CKV:_T_CKV + KV_RANK]
    c = c * lax.rsqrt(jnp.mean(c * c, axis=-1, keepdims=True) + NORM_EPS) * kvg_ref[...]
    c = c.astype(BF16)
    kn = jnp.dot(c, wuk_ref[...], preferred_element_type=F32)
    vt = lax.dot_general(wuvt_ref[...], c, _NT, preferred_element_type=F32)
    for ci in range(PROJ_TILE // KV_TILE):
        vat_ref[ci] = vt[:, ci * KV_TILE:(ci + 1) * KV_TILE].astype(BF16)
    kr = _rope_lanes(tok[:, _T_KR:_T_KR + LANES], ca, sa).astype(BF16)
    for hd in range(MLA_HEADS):
        base = hd * MLA_QK_PAD
        ka_ref[:, base:base + LANES] = kn[:, hd * LANES:(hd + 1) * LANES].astype(BF16)
        ka_ref[:, base + LANES:base + 2 * LANES] = kr
    ga_ref[...] = _silu(tok[:, _T_GA:_T_GA + 512]).astype(BF16)
    for hd in range(DIFF_HEADS):
        sl = slice(_T_DK + hd * LANES, _T_DK + (hd + 1) * LANES)
        dk_ref[:, hd * LANES:(hd + 1) * LANES] = _rope_lanes(tok[:, sl], cb, sb).astype(BF16)
    gb_ref[...] = _silu(tok[:, _T_GB:_T_GB + 512]).astype(BF16)


def _scores_t(k, qt, mask):
    s_t = jnp.dot(k, qt, preferred_element_type=F32)
    if mask is not None:
        s_t = jnp.where(mask, s_t, -jnp.inf)
    return s_t


def _softmax_pv(s_t, vt, m_ref, acc_ref, idx):
    m_old = m_ref[idx]
    m_new = jnp.maximum(m_old, jnp.max(s_t, axis=0, keepdims=True))
    alpha = jnp.exp2(m_old - m_new)
    p_t = jnp.exp2(s_t - m_new).astype(BF16)
    acc_ref[idx] = alpha * acc_ref[idx] + jnp.dot(vt, p_t, preferred_element_type=F32)
    m_ref[idx] = m_new


def _attn_kernel(qat_ref, dq1t_ref, dq2t_ref, ga_ref, gb_ref, x_ref,
                 ka_ref, vat_ref, dk_ref, dvt_ref, wout_ref,
                 lq1_ref, lk1_ref, lq2_ref, lk2_ref, subg_ref, postg_ref,
                 out_ref, ma_ref, acca_ref, mb_ref, accb_ref, *, lambda_init):
    qi = pl.program_id(1)
    ma_ref[...] = jnp.full(ma_ref.shape, -jnp.inf, F32)
    mb_ref[...] = jnp.full(mb_ref.shape, -jnp.inf, F32)
    acca_ref[...] = jnp.zeros(acca_ref.shape, F32)
    accb_ref[...] = jnp.zeros(accb_ref.shape, F32)

    ones = jnp.ones((BF16_ROWS, KV_TILE), BF16)
    key = lax.broadcasted_iota(jnp.int32, (KV_TILE, Q_TILE), 0)
    qry = lax.broadcasted_iota(jnp.int32, (KV_TILE, Q_TILE), 1)
    causal = key <= qry

    def kv_step(j, diagonal):
        rows = pl.ds(pl.multiple_of(j * KV_TILE, KV_TILE), KV_TILE)
        mask = causal if diagonal else None

        def mla_scores(h):
            cs = slice(h * MLA_QK_PAD, (h + 1) * MLA_QK_PAD)
            return _scores_t(ka_ref[rows, cs], qat_ref[cs, :], mask)

        def mla_update(h, s_t):
            vt = jnp.concatenate([vat_ref[j, h * MLA_V:(h + 1) * MLA_V, :], ones], axis=0)
            _softmax_pv(s_t, vt, ma_ref, acca_ref, h)

        def diff_scores(u):
            cs = slice((u // 2) * LANES, (u // 2 + 1) * LANES)
            qt_ref = dq2t_ref if u % 2 else dq1t_ref
            return _scores_t(dk_ref[rows, cs], qt_ref[cs, :], mask)

        def diff_update(u, s_t):
            cs = slice((u // 2) * LANES, (u // 2 + 1) * LANES)
            vt = jnp.concatenate([dvt_ref[j, cs, :], ones], axis=0)
            _softmax_pv(s_t, vt, mb_ref, accb_ref, u)

        units = [(mla_scores, mla_update, h) for h in range(MLA_HEADS)]
        units += [(diff_scores, diff_update, u) for u in range(2 * DIFF_HEADS)]
        pending = [scores(idx) for scores, _, idx in units[:SCORE_LOOKAHEAD]]
        for n, (_, update, idx) in enumerate(units):
            if n + SCORE_LOOKAHEAD < len(units):
                scores, _, nxt = units[n + SCORE_LOOKAHEAD]
                pending.append(scores(nxt))
            update(idx, pending.pop(0))

    def body(j, carry):
        kv_step(j, False)
        return carry

    lax.fori_loop(0, qi, body, 0)
    kv_step(qi, True)

    oa = []
    for h in range(MLA_HEADS):
        acc = acca_ref[h]
        o_t = acc[:MLA_V] * (1.0 / acc[MLA_V:MLA_V + 1])
        oa.append(o_t.T)
    oa = jnp.concatenate(oa, axis=1) * ga_ref[...].astype(F32)

    lam = (jnp.exp(jnp.sum(lq1_ref[...] * lk1_ref[...], axis=1, keepdims=True))
           - jnp.exp(jnp.sum(lq2_ref[...] * lk2_ref[...], axis=1, keepdims=True))
           + lambda_init)
    ob = []
    for h in range(DIFF_HEADS):
        acc1, acc2 = accb_ref[2 * h], accb_ref[2 * h + 1]
        a1 = acc1[:DIFF_V] * (1.0 / acc1[DIFF_V:DIFF_V + 1])
        a2 = acc2[:DIFF_V] * (1.0 / acc2[DIFF_V:DIFF_V + 1])
        o_t = a1 - lam * a2
        o_t = o_t * lax.rsqrt(jnp.mean(o_t * o_t, axis=0, keepdims=True) + SUBLN_EPS)
        ob.append(o_t.T * subg_ref[...] * (1.0 - lambda_init))
    ob = jnp.concatenate(ob, axis=1) * gb_ref[...].astype(F32)

    mixed_in = jnp.concatenate([oa, ob], axis=1).astype(BF16)
    mixed = jnp.dot(mixed_in, wout_ref[...], preferred_element_type=F32)
    y = mixed * lax.rsqrt(jnp.mean(mixed * mixed, axis=-1, keepdims=True) + NORM_EPS)
    out_ref[...] = x_ref[...] + y * postg_ref[...]


def _layer(x3, ln_pre_g, w_in, kv_norm_g, w_uk, w_uv, lq1, lk1, lq2, lk2,
           subln_g, w_out, ln_post_g, tables, lambda_init):
    batch, seq, d_model = x3.shape
    w_tok = jnp.take(w_in, _TOK_COLS, axis=1).astype(BF16)
    w_feat = jnp.take(w_in, _FEAT_COLS, axis=1).T.astype(BF16)
    n_tok, n_feat = w_tok.shape[1], w_feat.shape[0]
    g2 = ln_pre_g.reshape(1, d_model)
    kvg2 = kv_norm_g.reshape(1, KV_RANK)
    wuk = w_uk.astype(BF16)
    wuvt = w_uv.T.astype(BF16)

    seq_tiles = seq // PROJ_TILE
    kv_per_tile = PROJ_TILE // KV_TILE
    n_kv = seq // KV_TILE
    tok_blk = lambda w: pl.BlockSpec((None, PROJ_TILE, w), lambda b, t: (b, t, 0))
    feat_blk = lambda r: pl.BlockSpec((None, r, PROJ_TILE), lambda b, t: (b, 0, t))
    vt_blk = pl.BlockSpec((None, kv_per_tile, 512, KV_TILE), lambda b, t: (b, t, 0, 0))
    full = lambda a: pl.BlockSpec(a.shape, lambda b, t: (0,) * a.ndim)
    tab_tok = pl.BlockSpec((PROJ_TILE, LANES), lambda b, t: (t, 0))
    tab_feat = lambda r: pl.BlockSpec((r, PROJ_TILE), lambda b, t: (0, t))
    ca, sa, cb, sb, cat, sat, cbt, sbt = tables
    sds = jax.ShapeDtypeStruct
    qat, dq1t, dq2t, vat, dvt, ka, dk, ga, gb = pl.pallas_call(
        _proj_kernel,
        grid=(batch, seq_tiles),
        in_specs=[tok_blk(d_model), full(g2), full(w_tok), full(w_feat), full(kvg2),
                  full(wuk), full(wuvt), tab_tok, tab_tok, tab_tok, tab_tok,
                  tab_feat(cat.shape[0]), tab_feat(sat.shape[0]),
                  tab_feat(cbt.shape[0]), tab_feat(sbt.shape[0])],
        out_specs=[feat_blk(1024), feat_blk(512), feat_blk(512), vt_blk, vt_blk,
                   tok_blk(1024), tok_blk(512), tok_blk(512), tok_blk(512)],
        out_shape=[sds((batch, 1024, seq), BF16), sds((batch, 512, seq), BF16),
                   sds((batch, 512, seq), BF16),
                   sds((batch, n_kv, 512, KV_TILE), BF16), sds((batch, n_kv, 512, KV_TILE), BF16),
                   sds((batch, seq, 1024), BF16), sds((batch, seq, 512), BF16),
                   sds((batch, seq, 512), BF16), sds((batch, seq, 512), BF16)],
        compiler_params=pltpu.CompilerParams(
            dimension_semantics=("arbitrary", "arbitrary"), vmem_limit_bytes=VMEM_LIMIT_BYTES),
        name="proj",
    )(x3, g2, w_tok, w_feat, kvg2, wuk, wuvt, ca, sa, cb, sb, cat, sat, cbt, sbt)

    q_tok = lambda w: pl.BlockSpec((None, Q_TILE, w), lambda b, i: (b, i, 0))
    q_feat = lambda r: pl.BlockSpec((None, r, Q_TILE), lambda b, i: (b, 0, i))
    kv_tok = lambda w: pl.BlockSpec((None, seq, w), lambda b, i: (b, 0, 0))
    kv_feat = pl.BlockSpec((None, n_kv, 512, KV_TILE), lambda b, i: (b, 0, 0, 0))
    fullb = lambda a: pl.BlockSpec(a.shape, lambda b, i: (0,) * a.ndim)
    wout = w_out.astype(BF16)
    lvecs = [v.reshape(1, DIFF_QK) for v in (lq1, lk1, lq2, lk2)]
    subg2 = subln_g.reshape(1, DIFF_V)
    postg2 = ln_post_g.reshape(1, d_model)
    return pl.pallas_call(
        functools.partial(_attn_kernel, lambda_init=lambda_init),
        grid=(batch, seq // Q_TILE),
        in_specs=[q_feat(1024), q_feat(512), q_feat(512), q_tok(512), q_tok(512), q_tok(d_model),
                  kv_tok(1024), kv_feat, kv_tok(512), kv_feat, fullb(wout)]
                 + [fullb(v) for v in lvecs] + [fullb(subg2), fullb(postg2)],
        out_specs=q_tok(d_model),
        out_shape=sds((batch, seq, d_model), F32),
        scratch_shapes=[
            pltpu.VMEM((MLA_HEADS, 1, Q_TILE), F32),
            pltpu.VMEM((MLA_HEADS, ACC_ROWS, Q_TILE), F32),
            pltpu.VMEM((2 * DIFF_HEADS, 1, Q_TILE), F32),
            pltpu.VMEM((2 * DIFF_HEADS, ACC_ROWS, Q_TILE), F32),
        ],
        compiler_params=pltpu.CompilerParams(
            dimension_semantics=("arbitrary", "arbitrary"), vmem_limit_bytes=VMEM_LIMIT_BYTES),
        name="attn",
    )(qat, dq1t, dq2t, ga, gb, x3, ka, vat, dk, dvt, wout, *lvecs, subg2, postg2)


def kernel(x, ln_pre_g, w_in, kv_norm_g, w_uk, w_uv, lambda_q1, lambda_k1,
           lambda_q2, lambda_k2, subln_g, w_out, ln_post_g):
    batch, seq, d_model = x.shape
    depth = w_in.shape[0]
    assert seq % PROJ_TILE == 0 and PROJ_TILE % KV_TILE == 0 and Q_TILE == KV_TILE
    assert w_in.shape[2] == _TOK_COLS.size + _FEAT_COLS.size - MLA_ROPE and d_model == 1024
    tables = _rope_tables(seq)
    for l in range(depth):
        lambda_init = 0.8 - 0.6 * math.exp(-0.3 * l)
        x = _layer(x, ln_pre_g[l], w_in[l], kv_norm_g[l], w_uk[l], w_uv[l],
                   lambda_q1[l], lambda_k1[l], lambda_q2[l], lambda_k2[l], subln_g[l],
                   w_out[l], ln_post_g[l], tables, lambda_init)
    return x
```

```python
import functools
import math

import jax
import jax.numpy as jnp
import numpy as np
from jax import lax
from jax.experimental import pallas as pl
from jax.experimental.pallas import tpu as pltpu

ROPE_THETA = 500000.0
NORM_EPS = 1e-6
SUBLN_EPS = 1e-5

MLA_HEADS = 4
MLA_NOPE = 128
MLA_ROPE = 64
MLA_V = 128
KV_RANK = 128
DIFF_HEADS = 4
DIFF_QK = 64
DIFF_V = 2 * DIFF_QK
DIFF_ROPE = DIFF_QK // 4

LANES = 128
BF16_ROWS = 16
MLA_QK_PAD = 2 * LANES
LOG2E = math.log2(math.e)
MLA_QSCALE = LOG2E / math.sqrt(MLA_NOPE + MLA_ROPE)
DIFF_QSCALE = LOG2E / math.sqrt(DIFF_QK)

PROJ_TILE = 512
KV_TILE = 256
Q_TILE = 2 * KV_TILE
N_MAPS = MLA_HEADS + 2 * DIFF_HEADS
FINAL_ROWS = Q_TILE
ACC_ROWS = MLA_V + BF16_ROWS
SCORE_LOOKAHEAD = 2
VMEM_LIMIT_BYTES = 56 * 1024 * 1024

BF16 = jnp.bfloat16
F32 = jnp.float32
_NT = (((1,), (1,)), ((), ()))


def _w_in_layout():
    q_nope = 0
    q_rope = q_nope + MLA_HEADS * MLA_NOPE
    c_kv = q_rope + MLA_HEADS * MLA_ROPE
    k_rope = c_kv + KV_RANK
    g_a = k_rope + MLA_ROPE
    dq = g_a + MLA_HEADS * MLA_V
    dk = dq + DIFF_HEADS * 2 * DIFF_QK
    dv = dk + DIFF_HEADS * 2 * DIFF_QK
    g_b = dv + DIFF_HEADS * DIFF_V
    end = g_b + DIFF_HEADS * DIFF_V
    half = MLA_ROPE // 2
    rh = DIFF_ROPE // 2

    def diff_perm(base):
        cols = []
        for h in range(DIFF_HEADS):
            m0 = base + h * 2 * DIFF_QK
            m1 = m0 + DIFF_QK
            cols += list(range(m0, m0 + rh)) + list(range(m1, m1 + rh))
            cols += list(range(m0 + 2 * rh, m0 + DIFF_QK))
            cols += list(range(m0 + rh, m0 + 2 * rh)) + list(range(m1 + rh, m1 + 2 * rh))
            cols += list(range(m1 + 2 * rh, m1 + DIFF_QK))
        return cols

    x1 = list(range(k_rope, k_rope + half))
    x2 = list(range(k_rope + half, k_rope + 2 * half))
    tok = list(range(c_kv, k_rope)) + x1 + x1 + x2 + x2
    tok += list(range(g_a, dq)) + diff_perm(dk) + list(range(g_b, end))
    feat = list(range(q_nope, c_kv)) + diff_perm(dq) + list(range(dv, g_b))
    return np.asarray(tok, np.int32), np.asarray(feat, np.int32)


_TOK_COLS, _FEAT_COLS = _w_in_layout()
_T_CKV, _T_KR, _T_GA, _T_DK, _T_GB = 0, 128, 256, 768, 1280
_F_QN, _F_QR, _F_DQ, _F_DV = 0, 512, 768, 1280


def _rope_tables(seq):
    def tab(dim):
        inv_freq = ROPE_THETA ** (-jnp.arange(0, dim, 2, dtype=F32) / dim)
        ang = jnp.arange(seq, dtype=F32)[:, None] * inv_freq[None, :]
        return jnp.cos(ang), jnp.sin(ang)
    ca, sa = tab(MLA_ROPE)
    cb, sb = tab(DIFF_ROPE)
    one = jnp.ones((seq, LANES // 2 - 2 * cb.shape[1]), F32)
    zero = jnp.zeros_like(one)
    tok = (jnp.concatenate([ca, ca, ca, ca], axis=1),
           jnp.concatenate([-sa, -sa, sa, sa], axis=1),
           jnp.concatenate([cb, cb, one, cb, cb, one], axis=1),
           jnp.concatenate([-sb, -sb, zero, sb, sb, zero], axis=1))
    feat = (ca.T, sa.T, jnp.concatenate([cb, cb], axis=1).T, jnp.concatenate([sb, sb], axis=1).T)
    return tok + feat


def _silu(g):
    return g / (1.0 + jnp.exp(-g))


def _rope_lanes(v, cos, sin):
    return v * cos + pltpu.roll(v, LANES // 2, axis=1) * sin


def _proj_kernel(x_ref, g_ref, wtok_ref, wfeat_ref, kvg_ref, wuk_ref, wuvt_ref,
                 ca_ref, sa_ref, cb_ref, sb_ref, cat_ref, sat_ref, cbt_ref, sbt_ref,
                 qat_ref, dq1t_ref, dq2t_ref, vat_ref, dvt_ref, ka_ref, dk_ref, ga_ref, gb_ref):
    x = x_ref[...]
    ms = jnp.mean(x * x, axis=-1, keepdims=True)
    h = (x * lax.rsqrt(ms + NORM_EPS) * g_ref[...]).astype(BF16)

    feat = lax.dot_general(wfeat_ref[...], h, _NT, preferred_element_type=F32)
    cat, sat = cat_ref[...], sat_ref[...]
    half = MLA_ROPE // 2
    zpad = jnp.zeros((half, PROJ_TILE), BF16)
    for hd in range(MLA_HEADS):
        base = hd * MLA_QK_PAD
        qn = feat[_F_QN + hd * MLA_NOPE:_F_QN + (hd + 1) * MLA_NOPE]
        qat_ref[base:base + MLA_NOPE, :] = (qn * MLA_QSCALE).astype(BF16)
        x1 = feat[_F_QR + hd * MLA_ROPE:_F_QR + hd * MLA_ROPE + half]
        x2 = feat[_F_QR + hd * MLA_ROPE + half:_F_QR + (hd + 1) * MLA_ROPE]
        r1 = (x1 * cat - x2 * sat) * MLA_QSCALE
        r2 = (x2 * cat + x1 * sat) * MLA_QSCALE
        rope_rows = jnp.concatenate([r1.astype(BF16), zpad, r2.astype(BF16), zpad], axis=0)
        qat_ref[base + MLA_NOPE:base + MLA_QK_PAD, :] = rope_rows

    cbt, sbt = cbt_ref[...], sbt_ref[...]
    rh = DIFF_ROPE // 2
    frow = lax.broadcasted_iota(jnp.int32, (LANES, PROJ_TILE), 0)
    first_map = ((frow % (LANES // 2)) < rh) | ((frow >= 2 * rh) & (frow < LANES // 2))
    for hd in range(DIFF_HEADS):
        d = feat[_F_DQ + hd * LANES:_F_DQ + (hd + 1) * LANES]
        x1, x2 = d[:2 * rh], d[LANES // 2:LANES // 2 + 2 * rh]
        q = jnp.concatenate([x1 * cbt - x2 * sbt, d[2 * rh:LANES // 2],
                             x2 * cbt + x1 * sbt, d[LANES // 2 + 2 * rh:]], axis=0) * DIFF_QSCALE
        sl = slice(hd * LANES, (hd + 1) * LANES)
        dq1t_ref[sl, :] = jnp.where(first_map, q, 0.0).astype(BF16)
        dq2t_ref[sl, :] = jnp.where(first_map, 0.0, q).astype(BF16)
    for c in range(PROJ_TILE // KV_TILE):
        dvt_ref[c] = feat[_F_DV:, c * KV_TILE:(c + 1) * KV_TILE].astype(BF16)

    tok = jnp.dot(h, wtok_ref[...], preferred_element_type=F32)
    ca, sa, cb, sb = ca_ref[...], sa_ref[...], cb_ref[...], sb_ref[...]
    c = tok[:, _T_CKV:_T_CKV + KV_RANK]
    c = c * lax.rsqrt(jnp.mean(c * c, axis=-1, keepdims=True) + NORM_EPS) * kvg_ref[...]
    c = c.astype(BF16)
    kn = jnp.dot(c, wuk_ref[...], preferred_element_type=F32)
    vt = lax.dot_general(wuvt_ref[...], c, _NT, preferred_element_type=F32)
    for ci in range(PROJ_TILE // KV_TILE):
        vat_ref[ci] = vt[:, ci * KV_TILE:(ci + 1) * KV_TILE].astype(BF16)
    kr = _rope_lanes(tok[:, _T_KR:_T_KR + LANES], ca, sa).astype(BF16)
    for hd in range(MLA_HEADS):
        base = hd * MLA_QK_PAD
        ka_ref[:, base:base + LANES] = kn[:, hd * LANES:(hd + 1) * LANES].astype(BF16)
        ka_ref[:, base + LANES:base + 2 * LANES] = kr
    ga_ref[...] = _silu(tok[:, _T_GA:_T_GA + 512]).astype(BF16)
    for hd in range(DIFF_HEADS):
        sl = slice(_T_DK + hd * LANES, _T_DK + (hd + 1) * LANES)
        dk_ref[:, hd * LANES:(hd + 1) * LANES] = _rope_lanes(tok[:, sl], cb, sb).astype(BF16)
    gb_ref[...] = _silu(tok[:, _T_GB:_T_GB + 512]).astype(BF16)


def _scores_t(k, qt, mask):
    s_t = jnp.dot(k, qt, preferred_element_type=F32)
    if mask is not None:
        s_t = jnp.where(mask, s_t, -jnp.inf)
    return s_t


def _softmax_pv(s_t, vt, m_ref, acc_ref, idx, lanes):
    m_old = m_ref[idx, :, lanes]
    m_new = jnp.maximum(m_old, jnp.max(s_t, axis=0, keepdims=True))
    alpha = jnp.exp2(m_old - m_new)
    p_t = jnp.exp2(s_t - m_new).astype(BF16)
    pv = jnp.dot(vt, p_t, preferred_element_type=F32)
    acc_ref[idx, :, lanes] = alpha * acc_ref[idx, :, lanes] + pv
    m_ref[idx, :, lanes] = m_new


def _run_pipelined(units):
    pending = [scores() for scores, _ in units[:SCORE_LOOKAHEAD]]
    for n, (_, update) in enumerate(units):
        if n + SCORE_LOOKAHEAD < len(units):
            pending.append(units[n + SCORE_LOOKAHEAD][0]())
        update(pending.pop(0))


def _attn_kernel(qat_ref, dq1t_ref, dq2t_ref, ga_ref, gb_ref, x_ref,
                 ka_ref, vat_ref, dk_ref, dvt_ref, wout_ref,
                 lq1_ref, lk1_ref, lq2_ref, lk2_ref, subg_ref, postg_ref,
                 out_ref, m_ref, acc_ref, *, lambda_init):
    qi = pl.program_id(1)
    m_ref[...] = jnp.full(m_ref.shape, -jnp.inf, F32)
    acc_ref[...] = jnp.zeros(acc_ref.shape, F32)

    ones = jnp.ones((BF16_ROWS, KV_TILE), BF16)
    key = lax.broadcasted_iota(jnp.int32, (KV_TILE, Q_TILE), 0)
    qry = lax.broadcasted_iota(jnp.int32, (KV_TILE, Q_TILE), 1)
    causal = key <= qry
    all_q = slice(0, Q_TILE)
    upper_q = slice(KV_TILE, Q_TILE)

    def tile_units(j, lanes, mask):
        rows = pl.ds(pl.multiple_of(j * KV_TILE, KV_TILE), KV_TILE)
        units = []
        for u in range(N_MAPS):
            if u < MLA_HEADS:
                cs = slice(u * MLA_QK_PAD, (u + 1) * MLA_QK_PAD)
                vs = slice(u * MLA_V, (u + 1) * MLA_V)
                k_ref, qt_ref, v_ref = ka_ref, qat_ref, vat_ref
            else:
                d = u - MLA_HEADS
                cs = vs = slice((d // 2) * LANES, (d // 2 + 1) * LANES)
                k_ref, qt_ref, v_ref = dk_ref, (dq2t_ref if d % 2 else dq1t_ref), dvt_ref

            def scores(cs=cs, k_ref=k_ref, qt_ref=qt_ref):
                return _scores_t(k_ref[rows, cs], qt_ref[cs, lanes], mask)

            def update(s_t, u=u, vs=vs, v_ref=v_ref):
                vt = jnp.concatenate([v_ref[j, vs, :], ones], axis=0)
                _softmax_pv(s_t, vt, m_ref, acc_ref, u, lanes)

            units.append((scores, update))
        return units

    def body(i, carry):
        _run_pipelined(tile_units(2 * i, all_q, None) + tile_units(2 * i + 1, all_q, None))
        return carry

    lax.fori_loop(0, qi, body, 0)
    _run_pipelined(tile_units(2 * qi, all_q, causal)
                   + tile_units(2 * qi + 1, upper_q, causal[:, :KV_TILE]))

    lam = (jnp.exp(jnp.sum(lq1_ref[...] * lk1_ref[...], axis=1, keepdims=True))
           - jnp.exp(jnp.sum(lq2_ref[...] * lk2_ref[...], axis=1, keepdims=True))
           + lambda_init)
    for c in range(Q_TILE // FINAL_ROWS):
        qs = slice(c * FINAL_ROWS, (c + 1) * FINAL_ROWS)

        def normalised(u):
            return acc_ref[u, :MLA_V, qs] * (1.0 / acc_ref[u, MLA_V:MLA_V + 1, qs])

        oa = jnp.concatenate([normalised(h).T for h in range(MLA_HEADS)], axis=1)
        oa = oa * ga_ref[qs, :].astype(F32)
        ob = []
        for h in range(DIFF_HEADS):
            o_t = normalised(MLA_HEADS + 2 * h) - lam * normalised(MLA_HEADS + 2 * h + 1)
            o_t = o_t * lax.rsqrt(jnp.mean(o_t * o_t, axis=0, keepdims=True) + SUBLN_EPS)
            ob.append(o_t.T * subg_ref[...] * (1.0 - lambda_init))
        ob = jnp.concatenate(ob, axis=1) * gb_ref[qs, :].astype(F32)

        mixed_in = jnp.concatenate([oa, ob], axis=1).astype(BF16)
        mixed = jnp.dot(mixed_in, wout_ref[...], preferred_element_type=F32)
        y = mixed * lax.rsqrt(jnp.mean(mixed * mixed, axis=-1, keepdims=True) + NORM_EPS)
        out_ref[qs, :] = x_ref[qs, :] + y * postg_ref[...]


def _layer(x3, ln_pre_g, w_in, kv_norm_g, w_uk, w_uv, lq1, lk1, lq2, lk2,
           subln_g, w_out, ln_post_g, tables, lambda_init):
    batch, seq, d_model = x3.shape
    w_tok = jnp.take(w_in, _TOK_COLS, axis=1).astype(BF16)
    w_feat = jnp.take(w_in, _FEAT_COLS, axis=1).T.astype(BF16)
    n_tok, n_feat = w_tok.shape[1], w_feat.shape[0]
    g2 = ln_pre_g.reshape(1, d_model)
    kvg2 = kv_norm_g.reshape(1, KV_RANK)
    wuk = w_uk.astype(BF16)
    wuvt = w_uv.T.astype(BF16)

    seq_tiles = seq // PROJ_TILE
    kv_per_tile = PROJ_TILE // KV_TILE
    n_kv = seq // KV_TILE
    tok_blk = lambda w: pl.BlockSpec((None, PROJ_TILE, w), lambda b, t: (b, t, 0))
    feat_blk = lambda r: pl.BlockSpec((None, r, PROJ_TILE), lambda b, t: (b, 0, t))
    vt_blk = pl.BlockSpec((None, kv_per_tile, 512, KV_TILE), lambda b, t: (b, t, 0, 0))
    full = lambda a: pl.BlockSpec(a.shape, lambda b, t: (0,) * a.ndim)
    tab_tok = pl.BlockSpec((PROJ_TILE, LANES), lambda b, t: (t, 0))
    tab_feat = lambda r: pl.BlockSpec((r, PROJ_TILE), lambda b, t: (0, t))
    ca, sa, cb, sb, cat, sat, cbt, sbt = tables
    sds = jax.ShapeDtypeStruct
    qat, dq1t, dq2t, vat, dvt, ka, dk, ga, gb = pl.pallas_call(
        _proj_kernel,
        grid=(batch, seq_tiles),
        in_specs=[tok_blk(d_model), full(g2), full(w_tok), full(w_feat), full(kvg2),
                  full(wuk), full(wuvt), tab_tok, tab_tok, tab_tok, tab_tok,
                  tab_feat(cat.shape[0]), tab_feat(sat.shape[0]),
                  tab_feat(cbt.shape[0]), tab_feat(sbt.shape[0])],
        out_specs=[feat_blk(1024), feat_blk(512), feat_blk(512), vt_blk, vt_blk,
                   tok_blk(1024), tok_blk(512), tok_blk(512), tok_blk(512)],
        out_shape=[sds((batch, 1024, seq), BF16), sds((batch, 512, seq), BF16),
                   sds((batch, 512, seq), BF16),
                   sds((batch, n_kv, 512, KV_TILE), BF16), sds((batch, n_kv, 512, KV_TILE), BF16),
                   sds((batch, seq, 1024), BF16), sds((batch, seq, 512), BF16),
                   sds((batch, seq, 512), BF16), sds((batch, seq, 512), BF16)],
        compiler_params=pltpu.CompilerParams(
            dimension_semantics=("arbitrary", "arbitrary"), vmem_limit_bytes=VMEM_LIMIT_BYTES),
        name="proj",
    )(x3, g2, w_tok, w_feat, kvg2, wuk, wuvt, ca, sa, cb, sb, cat, sat, cbt, sbt)

    q_tok = lambda w: pl.BlockSpec((None, Q_TILE, w), lambda b, i: (b, i, 0))
    q_feat = lambda r: pl.BlockSpec((None, r, Q_TILE), lambda b, i: (b, 0, i))
    kv_tok = lambda w: pl.BlockSpec((None, seq, w), lambda b, i: (b, 0, 0))
    kv_feat = pl.BlockSpec((None, n_kv, 512, KV_TILE), lambda b, i: (b, 0, 0, 0))
    fullb = lambda a: pl.BlockSpec(a.shape, lambda b, i: (0,) * a.ndim)
    wout = w_out.astype(BF16)
    lvecs = [v.reshape(1, DIFF_QK) for v in (lq1, lk1, lq2, lk2)]
    subg2 = subln_g.reshape(1, DIFF_V)
    postg2 = ln_post_g.reshape(1, d_model)
    return pl.pallas_call(
        functools.partial(_attn_kernel, lambda_init=lambda_init),
        grid=(batch, seq // Q_TILE),
        in_specs=[q_feat(1024), q_feat(512), q_feat(512), q_tok(512), q_tok(512), q_tok(d_model),
                  kv_tok(1024), kv_feat, kv_tok(512), kv_feat, fullb(wout)]
                 + [fullb(v) for v in lvecs] + [fullb(subg2), fullb(postg2)],
        out_specs=q_tok(d_model),
        out_shape=sds((batch, seq, d_model), F32),
        scratch_shapes=[
            pltpu.VMEM((N_MAPS, 1, Q_TILE), F32),
            pltpu.VMEM((N_MAPS, ACC_ROWS, Q_TILE), F32),
        ],
        compiler_params=pltpu.CompilerParams(
            dimension_semantics=("arbitrary", "arbitrary"), vmem_limit_bytes=VMEM_LIMIT_BYTES),
        name="attn",
    )(qat, dq1t, dq2t, ga, gb, x3, ka, vat, dk, dvt, wout, *lvecs, subg2, postg2)


def kernel(x, ln_pre_g, w_in, kv_norm_g, w_uk, w_uv, lambda_q1, lambda_k1,
           lambda_q2, lambda_k2, subln_g, w_out, ln_post_g):
    batch, seq, d_model = x.shape
    depth = w_in.shape[0]
    assert seq % PROJ_TILE == 0 and PROJ_TILE % KV_TILE == 0 and seq % Q_TILE == 0
    assert w_in.shape[2] == _TOK_COLS.size + _FEAT_COLS.size - MLA_ROPE and d_model == 1024
    tables = _rope_tables(seq)
    for l in range(depth):
        lambda_init = 0.8 - 0.6 * math.exp(-0.3 * l)
        x = _layer(x, ln_pre_g[l], w_in[l], kv_norm_g[l], w_uk[l], w_uv[l],
                   lambda_q1[l], lambda_k1[l], lambda_q2[l], lambda_k2[l], subln_g[l],
                   w_out[l], ln_post_g[l], tables, lambda_init)
    return x
```

```python
import functools
import math

import jax
import jax.numpy as jnp
import numpy as np
from jax import lax
from jax.experimental import pallas as pl
from jax.experimental.pallas import tpu as pltpu

ROPE_THETA = 500000.0
NORM_EPS = 1e-6
SUBLN_EPS = 1e-5

MLA_HEADS = 4
MLA_NOPE = 128
MLA_ROPE = 64
MLA_V = 128
KV_RANK = 128
DIFF_HEADS = 4
DIFF_QK = 64
DIFF_V = 2 * DIFF_QK
DIFF_ROPE = DIFF_QK // 4

LANES = 128
BF16_ROWS = 16
MLA_QK_PAD = 2 * LANES
LOG2E = math.log2(math.e)
MLA_QSCALE = LOG2E / math.sqrt(MLA_NOPE + MLA_ROPE)
DIFF_QSCALE = LOG2E / math.sqrt(DIFF_QK)

PROJ_TILE = 512
KV_TILE = 256
Q_TILE = 2 * KV_TILE
N_MAPS = MLA_HEADS + 2 * DIFF_HEADS
ACC_ROWS = MLA_V + BF16_ROWS
SCORE_LOOKAHEAD = 2
VMEM_LIMIT_BYTES = 56 * 1024 * 1024

BF16 = jnp.bfloat16
F32 = jnp.float32
_NT = (((1,), (1,)), ((), ()))

_C_QN = 0
_C_QR = _C_QN + MLA_HEADS * MLA_NOPE
_C_CKV = _C_QR + MLA_HEADS * MLA_ROPE
_C_KR = _C_CKV + KV_RANK
_C_GA = _C_KR + MLA_ROPE
_C_DQ = _C_GA + MLA_HEADS * MLA_V
_C_DK = _C_DQ + DIFF_HEADS * 2 * DIFF_QK
_C_DV = _C_DK + DIFF_HEADS * 2 * DIFF_QK
_C_GB = _C_DV + DIFF_HEADS * DIFF_V
_C_END = _C_GB + DIFF_HEADS * DIFF_V
_T_CKV, _T_KR, _T_GA, _T_DK, _T_GB = 0, 128, 256, 768, 1280
_F_QN, _F_QR, _F_DQ, _F_DV = 0, 512, 768, 1280


def _split_weights(w_in):
    pad = jnp.zeros((w_in.shape[0], LANES - MLA_ROPE), w_in.dtype)
    w_tok = jnp.concatenate([w_in[:, _C_CKV:_C_GA], pad, w_in[:, _C_GA:_C_DQ],
                             w_in[:, _C_DK:_C_DV], w_in[:, _C_GB:_C_END]], axis=1)
    w_feat = jnp.concatenate([w_in[:, _C_QN:_C_CKV], w_in[:, _C_DQ:_C_DK],
                              w_in[:, _C_DV:_C_GB]], axis=1).T
    return w_tok.astype(BF16), w_feat.astype(BF16)


def _rope_tables(seq):
    def tab(dim):
        inv_freq = ROPE_THETA ** (-np.arange(0, dim, 2, dtype=np.float64) / dim)
        ang = np.arange(seq, dtype=np.float64)[:, None] * inv_freq[None, :]
        return np.cos(ang), np.sin(ang)

    def lane_tables(c, s, group, rest_cos):
        half = c.shape[1]
        rest = np.full((seq, group - 2 * half), rest_cos)
        zero = np.zeros((seq, group - 2 * half))
        zh = np.zeros((seq, half))
        reps = LANES // group
        return (np.concatenate([c, c, rest] * reps, axis=1),
                np.concatenate([-s, zh, zero] * reps, axis=1),
                np.concatenate([zh, s, zero] * reps, axis=1))

    ca, sa = tab(MLA_ROPE)
    cb, sb = tab(DIFF_ROPE)
    tables = lane_tables(ca, sa, LANES, 0.0)
    tables += lane_tables(cb, sb, DIFF_QK, 1.0)
    tables += (ca.T, sa.T, cb.T, sb.T)
    return tuple(jnp.asarray(t, F32) for t in tables)


def _silu(g):
    return g / (1.0 + jnp.exp(-g))


def _rope_lanes(v, cos, sin_up, sin_down, half):
    return (v * cos + pltpu.roll(v, LANES - half, axis=1) * sin_up
            + pltpu.roll(v, half, axis=1) * sin_down)


def _proj_kernel(x_ref, g_ref, wtok_ref, wfeat_ref, kvg_ref, wuk_ref, wuvt_ref,
                 ca_ref, sau_ref, sad_ref, cb_ref, sbu_ref, sbd_ref,
                 cat_ref, sat_ref, cbt_ref, sbt_ref,
                 qat_ref, dq1t_ref, dq2t_ref, vat_ref, dvt_ref, ka_ref, dk_ref, ga_ref, gb_ref):
    x = x_ref[...]
    ms = jnp.mean(x * x, axis=-1, keepdims=True)
    h = (x * lax.rsqrt(ms + NORM_EPS) * g_ref[...]).astype(BF16)

    feat = lax.dot_general(wfeat_ref[...], h, _NT, preferred_element_type=F32)
    cat, sat = cat_ref[...], sat_ref[...]
    half = MLA_ROPE // 2
    zpad = jnp.zeros((MLA_QK_PAD - MLA_NOPE - MLA_ROPE, PROJ_TILE), BF16)
    for hd in range(MLA_HEADS):
        base = hd * MLA_QK_PAD
        qn = feat[_F_QN + hd * MLA_NOPE:_F_QN + (hd + 1) * MLA_NOPE]
        qat_ref[base:base + MLA_NOPE, :] = (qn * MLA_QSCALE).astype(BF16)
        x1 = feat[_F_QR + hd * MLA_ROPE:_F_QR + hd * MLA_ROPE + half]
        x2 = feat[_F_QR + hd * MLA_ROPE + half:_F_QR + (hd + 1) * MLA_ROPE]
        r1 = (x1 * cat - x2 * sat) * MLA_QSCALE
        r2 = (x2 * cat + x1 * sat) * MLA_QSCALE
        rope_rows = jnp.concatenate([r1.astype(BF16), r2.astype(BF16), zpad], axis=0)
        qat_ref[base + MLA_NOPE:base + MLA_QK_PAD, :] = rope_rows

    cbt, sbt = cbt_ref[...], sbt_ref[...]
    rh = DIFF_ROPE // 2
    zmap = jnp.zeros((DIFF_QK, PROJ_TILE), BF16)
    for hd in range(DIFF_HEADS):
        maps = []
        for mp in range(2):
            d = feat[_F_DQ + (2 * hd + mp) * DIFF_QK:_F_DQ + (2 * hd + mp + 1) * DIFF_QK]
            x1, x2 = d[:rh], d[rh:2 * rh]
            q = jnp.concatenate([x1 * cbt - x2 * sbt, x2 * cbt + x1 * sbt, d[2 * rh:]], axis=0)
            maps.append((q * DIFF_QSCALE).astype(BF16))
        sl = slice(hd * LANES, (hd + 1) * LANES)
        dq1t_ref[sl, :] = jnp.concatenate([maps[0], zmap], axis=0)
        dq2t_ref[sl, :] = jnp.concatenate([zmap, maps[1]], axis=0)
    for c in range(PROJ_TILE // KV_TILE):
        dvt_ref[c] = feat[_F_DV:, c * KV_TILE:(c + 1) * KV_TILE].astype(BF16)

    tok = jnp.dot(h, wtok_ref[...], preferred_element_type=F32)
    c = tok[:, _T_CKV:_T_CKV + KV_RANK]
    c = c * lax.rsqrt(jnp.mean(c * c, axis=-1, keepdims=True) + NORM_EPS) * kvg_ref[...]
    c = c.astype(BF16)
    kn = jnp.dot(c, wuk_ref[...], preferred_element_type=F32)
    vt = lax.dot_general(wuvt_ref[...], c, _NT, preferred_element_type=F32)
    for ci in range(PROJ_TILE // KV_TILE):
        vat_ref[ci] = vt[:, ci * KV_TILE:(ci + 1) * KV_TILE].astype(BF16)
    kr = _rope_lanes(tok[:, _T_KR:_T_KR + LANES], ca_ref[...], sau_ref[...], sad_ref[...], half)
    kr = kr.astype(BF16)
    for hd in range(MLA_HEADS):
        base = hd * MLA_QK_PAD
        ka_ref[:, base:base + LANES] = kn[:, hd * LANES:(hd + 1) * LANES].astype(BF16)
        ka_ref[:, base + LANES:base + 2 * LANES] = kr
    ga_ref[...] = _silu(tok[:, _T_GA:_T_GA + 512]).astype(BF16)
    cb, sbu, sbd = cb_ref[...], sbu_ref[...], sbd_ref[...]
    for hd in range(DIFF_HEADS):
        sl = slice(_T_DK + hd * LANES, _T_DK + (hd + 1) * LANES)
        dk_ref[:, hd * LANES:(hd + 1) * LANES] = _rope_lanes(tok[:, sl], cb, sbu, sbd, rh).astype(BF16)
    gb_ref[...] = _silu(tok[:, _T_GB:_T_GB + 512]).astype(BF16)


def _scores_t(k, qt, mask):
    s_t = jnp.dot(k, qt, preferred_element_type=F32)
    if mask is not None:
        s_t = jnp.where(mask, s_t, -jnp.inf)
    return s_t


def _softmax_pv(s_t, vt, m_ref, acc_ref, idx, lanes):
    m_old = m_ref[idx, :, lanes]
    m_new = jnp.maximum(m_old, jnp.max(s_t, axis=0, keepdims=True))
    alpha = jnp.exp2(m_old - m_new)
    p_t = jnp.exp2(s_t - m_new).astype(BF16)
    pv = jnp.dot(vt, p_t, preferred_element_type=F32)
    acc_ref[idx, :, lanes] = alpha * acc_ref[idx, :, lanes] + pv
    m_ref[idx, :, lanes] = m_new


def _run_pipelined(units):
    pending = [scores() for scores, _ in units[:SCORE_LOOKAHEAD]]
    for n, (_, update) in enumerate(units):
        if n + SCORE_LOOKAHEAD < len(units):
            pending.append(units[n + SCORE_LOOKAHEAD][0]())
        update(pending.pop(0))


def _attn_kernel(qat_ref, dq1t_ref, dq2t_ref, ga_ref, gb_ref, x_ref,
                 ka_ref, vat_ref, dk_ref, dvt_ref, wout_ref,
                 lq1_ref, lk1_ref, lq2_ref, lk2_ref, subg_ref, postg_ref,
                 out_ref, m_ref, acc_ref, *, lambda_init):
    qi = pl.program_id(1)
    m_ref[...] = jnp.full(m_ref.shape, -jnp.inf, F32)
    acc_ref[...] = jnp.zeros(acc_ref.shape, F32)

    ones = jnp.ones((BF16_ROWS, KV_TILE), BF16)
    key = lax.broadcasted_iota(jnp.int32, (KV_TILE, Q_TILE), 0)
    qry = lax.broadcasted_iota(jnp.int32, (KV_TILE, Q_TILE), 1)
    causal = key <= qry
    all_q = slice(0, Q_TILE)
    upper_q = slice(KV_TILE, Q_TILE)

    def tile_units(j, lanes, mask):
        rows = pl.ds(pl.multiple_of(j * KV_TILE, KV_TILE), KV_TILE)
        units = []
        for u in range(N_MAPS):
            if u < MLA_HEADS:
                cs = slice(u * MLA_QK_PAD, (u + 1) * MLA_QK_PAD)
                vs = slice(u * MLA_V, (u + 1) * MLA_V)
                k_ref, qt_ref, v_ref = ka_ref, qat_ref, vat_ref
            else:
                d = u - MLA_HEADS
                cs = vs = slice((d // 2) * LANES, (d // 2 + 1) * LANES)
                k_ref, qt_ref, v_ref = dk_ref, (dq2t_ref if d % 2 else dq1t_ref), dvt_ref

            def scores(cs=cs, k_ref=k_ref, qt_ref=qt_ref):
                return _scores_t(k_ref[rows, cs], qt_ref[cs, lanes], mask)

            def update(s_t, u=u, vs=vs, v_ref=v_ref):
                vt = jnp.concatenate([v_ref[j, vs, :], ones], axis=0)
                _softmax_pv(s_t, vt, m_ref, acc_ref, u, lanes)

            units.append((scores, update))
        return units

    def body(i, carry):
        _run_pipelined(tile_units(2 * i, all_q, None) + tile_units(2 * i + 1, all_q, None))
        return carry

    lax.fori_loop(0, qi, body, 0)
    _run_pipelined(tile_units(2 * qi, all_q, causal)
                   + tile_units(2 * qi + 1, upper_q, causal[:, :KV_TILE]))

    def normalised(u):
        acc = acc_ref[u]
        return acc[:MLA_V] * (1.0 / acc[MLA_V:MLA_V + 1])

    oa = jnp.concatenate([normalised(h).T for h in range(MLA_HEADS)], axis=1)
    oa = oa * ga_ref[...].astype(F32)
    lam = (jnp.exp(jnp.sum(lq1_ref[...] * lk1_ref[...], axis=1, keepdims=True))
           - jnp.exp(jnp.sum(lq2_ref[...] * lk2_ref[...], axis=1, keepdims=True))
           + lambda_init)
    ob = []
    for h in range(DIFF_HEADS):
        o_t = normalised(MLA_HEADS + 2 * h) - lam * normalised(MLA_HEADS + 2 * h + 1)
        o_t = o_t * lax.rsqrt(jnp.mean(o_t * o_t, axis=0, keepdims=True) + SUBLN_EPS)
        ob.append(o_t.T * subg_ref[...] * (1.0 - lambda_init))
    ob = jnp.concatenate(ob, axis=1) * gb_ref[...].astype(F32)

    mixed_in = jnp.concatenate([oa, ob], axis=1).astype(BF16)
    mixed = jnp.dot(mixed_in, wout_ref[...], preferred_element_type=F32)
    y = mixed * lax.rsqrt(jnp.mean(mixed * mixed, axis=-1, keepdims=True) + NORM_EPS)
    out_ref[...] = x_ref[...] + y * postg_ref[...]


def _layer(x3, ln_pre_g, w_in, kv_norm_g, w_uk, w_uv, lq1, lk1, lq2, lk2,
           subln_g, w_out, ln_post_g, tables, lambda_init):
    batch, seq, d_model = x3.shape
    w_tok, w_feat = _split_weights(w_in)
    g2 = ln_pre_g.reshape(1, d_model)
    kvg2 = kv_norm_g.reshape(1, KV_RANK)
    wuk = w_uk.astype(BF16)
    wuvt = w_uv.T.astype(BF16)

    seq_tiles = seq // PROJ_TILE
    kv_per_tile = PROJ_TILE // KV_TILE
    n_kv = seq // KV_TILE
    tok_blk = lambda w: pl.BlockSpec((None, PROJ_TILE, w), lambda b, t: (b, t, 0))
    feat_blk = lambda r: pl.BlockSpec((None, r, PROJ_TILE), lambda b, t: (b, 0, t))
    vt_blk = pl.BlockSpec((None, kv_per_tile, 512, KV_TILE), lambda b, t: (b, t, 0, 0))
    full = lambda a: pl.BlockSpec(a.shape, lambda b, t: (0,) * a.ndim)
    tab_tok = pl.BlockSpec((PROJ_TILE, LANES), lambda b, t: (t, 0))
    tab_feat = lambda a: pl.BlockSpec((a.shape[0], PROJ_TILE), lambda b, t: (0, t))
    tok_tabs, feat_tabs = tables[:6], tables[6:]
    sds = jax.ShapeDtypeStruct
    qat, dq1t, dq2t, vat, dvt, ka, dk, ga, gb = pl.pallas_call(
        _proj_kernel,
        grid=(batch, seq_tiles),
        in_specs=[tok_blk(d_model), full(g2), full(w_tok), full(w_feat), full(kvg2),
                  full(wuk), full(wuvt)] + [tab_tok] * 6 + [tab_feat(a) for a in feat_tabs],
        out_specs=[feat_blk(1024), feat_blk(512), feat_blk(512), vt_blk, vt_blk,
                   tok_blk(1024), tok_blk(512), tok_blk(512), tok_blk(512)],
        out_shape=[sds((batch, 1024, seq), BF16), sds((batch, 512, seq), BF16),
                   sds((batch, 512, seq), BF16),
                   sds((batch, n_kv, 512, KV_TILE), BF16), sds((batch, n_kv, 512, KV_TILE), BF16),
                   sds((batch, seq, 1024), BF16), sds((batch, seq, 512), BF16),
                   sds((batch, seq, 512), BF16), sds((batch, seq, 512), BF16)],
        compiler_params=pltpu.CompilerParams(
            dimension_semantics=("arbitrary", "arbitrary"), vmem_limit_bytes=VMEM_LIMIT_BYTES),
        name="proj",
    )(x3, g2, w_tok, w_feat, kvg2, wuk, wuvt, *tok_tabs, *feat_tabs)

    q_tok = lambda w: pl.BlockSpec((None, Q_TILE, w), lambda b, i: (b, i, 0))
    q_feat = lambda r: pl.BlockSpec((None, r, Q_TILE), lambda b, i: (b, 0, i))
    kv_tok = lambda w: pl.BlockSpec((None, seq, w), lambda b, i: (b, 0, 0))
    kv_feat = pl.BlockSpec((None, n_kv, 512, KV_TILE), lambda b, i: (b, 0, 0, 0))
    fullb = lambda a: pl.BlockSpec(a.shape, lambda b, i: (0,) * a.ndim)
    wout = w_out.astype(BF16)
    lvecs = [v.reshape(1, DIFF_QK) for v in (lq1, lk1, lq2, lk2)]
    subg2 = subln_g.reshape(1, DIFF_V)
    postg2 = ln_post_g.reshape(1, d_model)
    return pl.pallas_call(
        functools.partial(_attn_kernel, lambda_init=lambda_init),
        grid=(batch, seq // Q_TILE),
        in_specs=[q_feat(1024), q_feat(512), q_feat(512), q_tok(512), q_tok(512), q_tok(d_model),
                  kv_tok(1024), kv_feat, kv_tok(512), kv_feat, fullb(wout)]
                 + [fullb(v) for v in lvecs] + [fullb(subg2), fullb(postg2)],
        out_specs=q_tok(d_model),
        out_shape=sds((batch, seq, d_model), F32),
        scratch_shapes=[
            pltpu.VMEM((N_MAPS, 1, Q_TILE), F32),
            pltpu.VMEM((N_MAPS, ACC_ROWS, Q_TILE), F32),
        ],
        compiler_params=pltpu.CompilerParams(
            dimension_semantics=("arbitrary", "arbitrary"), vmem_limit_bytes=VMEM_LIMIT_BYTES),
        name="attn",
    )(qat, dq1t, dq2t, ga, gb, x3, ka, vat, dk, dvt, wout, *lvecs, subg2, postg2)


def kernel(x, ln_pre_g, w_in, kv_norm_g, w_uk, w_uv, lambda_q1, lambda_k1,
           lambda_q2, lambda_k2, subln_g, w_out, ln_post_g):
    batch, seq, d_model = x.shape
    depth = w_in.shape[0]
    assert seq % PROJ_TILE == 0 and PROJ_TILE % KV_TILE == 0 and seq % Q_TILE == 0
    assert w_in.shape[2] == _C_END and d_model == 1024
    tables = _rope_tables(seq)
    for l in range(depth):
        lambda_init = 0.8 - 0.6 * math.exp(-0.3 * l)
        x = _layer(x, ln_pre_g[l], w_in[l], kv_norm_g[l], w_uk[l], w_uv[l],
                   lambda_q1[l], lambda_k1[l], lambda_q2[l], lambda_k2[l], subln_g[l],
                   w_out[l], ln_post_g[l], tables, lambda_init)
    return x
```

```python
import functools
import math

import jax
import jax.numpy as jnp
import numpy as np
from jax import lax
from jax.experimental import pallas as pl
from jax.experimental.pallas import tpu as pltpu

ROPE_THETA = 500000.0
NORM_EPS = 1e-6
SUBLN_EPS = 1e-5

MLA_HEADS = 4
MLA_NOPE = 128
MLA_ROPE = 64
MLA_V = 128
KV_RANK = 128
DIFF_HEADS = 4
DIFF_QK = 64
DIFF_V = 2 * DIFF_QK
DIFF_ROPE = DIFF_QK // 4

LANES = 128
BF16_ROWS = 16
MLA_QK_PAD = 2 * LANES
LOG2E = math.log2(math.e)
MLA_QSCALE = LOG2E / math.sqrt(MLA_NOPE + MLA_ROPE)
DIFF_QSCALE = LOG2E / math.sqrt(DIFF_QK)

PROJ_TILE = 512
KV_TILE = 256
Q_TILE = 2 * KV_TILE
N_MAPS = MLA_HEADS + 2 * DIFF_HEADS
ACC_ROWS = MLA_V + BF16_ROWS
SCORE_LOOKAHEAD = 2
VMEM_LIMIT_BYTES = 56 * 1024 * 1024

BF16 = jnp.bfloat16
F32 = jnp.float32
_NT = (((1,), (1,)), ((), ()))

_C_QN = 0
_C_QR = _C_QN + MLA_HEADS * MLA_NOPE
_C_CKV = _C_QR + MLA_HEADS * MLA_ROPE
_C_KR = _C_CKV + KV_RANK
_C_GA = _C_KR + MLA_ROPE
_C_DQ = _C_GA + MLA_HEADS * MLA_V
_C_DK = _C_DQ + DIFF_HEADS * 2 * DIFF_QK
_C_DV = _C_DK + DIFF_HEADS * 2 * DIFF_QK
_C_GB = _C_DV + DIFF_HEADS * DIFF_V
_C_END = _C_GB + DIFF_HEADS * DIFF_V
_T_CKV, _T_KR, _T_GA, _T_DK, _T_GB = 0, 128, 256, 768, 1280
_F_QN, _F_QR, _F_DQ, _F_DV = 0, 512, 768, 1280


def _split_weights(w_in):
    pad = jnp.zeros((w_in.shape[0], LANES - MLA_ROPE), w_in.dtype)
    w_tok = jnp.concatenate([w_in[:, _C_CKV:_C_GA], pad, w_in[:, _C_GA:_C_DQ],
                             w_in[:, _C_DK:_C_DV], w_in[:, _C_GB:_C_END]], axis=1)
    w_feat = jnp.concatenate([w_in[:, _C_QN:_C_CKV], w_in[:, _C_DQ:_C_DK],
                              w_in[:, _C_DV:_C_GB]], axis=1).T
    return w_tok.astype(BF16), w_feat.astype(BF16)


def _rope_tables(seq):
    def tab(dim):
        inv_freq = ROPE_THETA ** (-np.arange(0, dim, 2, dtype=np.float64) / dim)
        ang = np.arange(seq, dtype=np.float64)[:, None] * inv_freq[None, :]
        return np.cos(ang), np.sin(ang)

    def lane_tables(c, s, group, rest_cos):
        half = c.shape[1]
        rest = np.full((seq, group - 2 * half), rest_cos)
        zero = np.zeros((seq, group - 2 * half))
        zh = np.zeros((seq, half))
        reps = LANES // group
        return (np.concatenate([c, c, rest] * reps, axis=1),
                np.concatenate([-s, zh, zero] * reps, axis=1),
                np.concatenate([zh, s, zero] * reps, axis=1))

    ca, sa = tab(MLA_ROPE)
    cb, sb = tab(DIFF_ROPE)
    tables = lane_tables(ca, sa, LANES, 0.0)
    tables += lane_tables(cb, sb, DIFF_QK, 1.0)
    tables += (ca.T, sa.T, cb.T, sb.T)
    return tuple(jnp.asarray(t, F32) for t in tables)


def _silu(g):
    return g / (1.0 + jnp.exp(-g))


def _rope_lanes(v, cos, sin_up, sin_down, half):
    return (v * cos + pltpu.roll(v, LANES - half, axis=1) * sin_up
            + pltpu.roll(v, half, axis=1) * sin_down)


def _proj_kernel(x_ref, g_ref, wtok_ref, wfeat_ref, kvg_ref, wuk_ref, wuvt_ref,
                 ca_ref, sau_ref, sad_ref, cb_ref, sbu_ref, sbd_ref,
                 cat_ref, sat_ref, cbt_ref, sbt_ref,
                 qat_ref, dq1t_ref, dq2t_ref, vat_ref, dvt_ref, ka_ref, dk_ref, ga_ref, gb_ref):
    x = x_ref[...]
    ms = jnp.mean(x * x, axis=-1, keepdims=True)
    h = (x * lax.rsqrt(ms + NORM_EPS) * g_ref[...]).astype(BF16)

    feat = lax.dot_general(wfeat_ref[...], h, _NT, preferred_element_type=F32)
    cat, sat = cat_ref[...], sat_ref[...]
    half = MLA_ROPE // 2
    zpad = jnp.zeros((MLA_QK_PAD - MLA_NOPE - MLA_ROPE, PROJ_TILE), BF16)
    for hd in range(MLA_HEADS):
        base = hd * MLA_QK_PAD
        qn = feat[_F_QN + hd * MLA_NOPE:_F_QN + (hd + 1) * MLA_NOPE]
        qat_ref[base:base + MLA_NOPE, :] = (qn * MLA_QSCALE).astype(BF16)
        x1 = feat[_F_QR + hd * MLA_ROPE:_F_QR + hd * MLA_ROPE + half]
        x2 = feat[_F_QR + hd * MLA_ROPE + half:_F_QR + (hd + 1) * MLA_ROPE]
        r1 = (x1 * cat - x2 * sat) * MLA_QSCALE
        r2 = (x2 * cat + x1 * sat) * MLA_QSCALE
        rope_rows = jnp.concatenate([r1.astype(BF16), r2.astype(BF16), zpad], axis=0)
        qat_ref[base + MLA_NOPE:base + MLA_QK_PAD, :] = rope_rows

    cbt, sbt = cbt_ref[...], sbt_ref[...]
    rh = DIFF_ROPE // 2
    zmap = jnp.zeros((DIFF_QK, PROJ_TILE), BF16)
    for hd in range(DIFF_HEADS):
        maps = []
        for mp in range(2):
            d = feat[_F_DQ + (2 * hd + mp) * DIFF_QK:_F_DQ + (2 * hd + mp + 1) * DIFF_QK]
            x1, x2 = d[:rh], d[rh:2 * rh]
            q = jnp.concatenate([x1 * cbt - x2 * sbt, x2 * cbt + x1 * sbt, d[2 * rh:]], axis=0)
            maps.append((q * DIFF_QSCALE).astype(BF16))
        sl = slice(hd * LANES, (hd + 1) * LANES)
        dq1t_ref[sl, :] = jnp.concatenate([maps[0], zmap], axis=0)
        dq2t_ref[sl, :] = jnp.concatenate([zmap, maps[1]], axis=0)
    for c in range(PROJ_TILE // KV_TILE):
        dvt_ref[c] = feat[_F_DV:, c * KV_TILE:(c + 1) * KV_TILE].astype(BF16)

    tok = jnp.dot(h, wtok_ref[...], preferred_element_type=F32)
    c = tok[:, _T_CKV:_T_CKV + KV_RANK]
    c = c * lax.rsqrt(jnp.mean(c * c, axis=-1, keepdims=True) + NORM_EPS) * kvg_ref[...]
    c = c.astype(BF16)
    kn = jnp.dot(c, wuk_ref[...], preferred_element_type=F32)
    vt = lax.dot_general(wuvt_ref[...], c, _NT, preferred_element_type=F32)
    for ci in range(PROJ_TILE // KV_TILE):
        vat_ref[ci] = vt[:, ci * KV_TILE:(ci + 1) * KV_TILE].astype(BF16)
    kr = _rope_lanes(tok[:, _T_KR:_T_KR + LANES], ca_ref[...], sau_ref[...], sad_ref[...], half)
    kr = kr.astype(BF16)
    for hd in range(MLA_HEADS):
        base = hd * MLA_QK_PAD
        ka_ref[:, base:base + LANES] = kn[:, hd * LANES:(hd + 1) * LANES].astype(BF16)
        ka_ref[:, base + LANES:base + 2 * LANES] = kr
    ga_ref[...] = _silu(tok[:, _T_GA:_T_GA + 512]).astype(BF16)
    cb, sbu, sbd = cb_ref[...], sbu_ref[...], sbd_ref[...]
    for hd in range(DIFF_HEADS):
        sl = slice(_T_DK + hd * LANES, _T_DK + (hd + 1) * LANES)
        dk_ref[:, hd * LANES:(hd + 1) * LANES] = _rope_lanes(tok[:, sl], cb, sbu, sbd, rh).astype(BF16)
    gb_ref[...] = _silu(tok[:, _T_GB:_T_GB + 512]).astype(BF16)


def _softmax_pv(s_t, vt, m_ref, acc_ref, idx, lanes):
    m_old = m_ref[idx, :, lanes]
    m_new = jnp.maximum(m_old, jnp.max(s_t, axis=0, keepdims=True))
    alpha = jnp.exp2(m_old - m_new)
    p_t = jnp.exp2(s_t - m_new).astype(BF16)
    pv = jnp.dot(vt, p_t, preferred_element_type=F32)
    acc_ref[idx, :, lanes] = alpha * acc_ref[idx, :, lanes] + pv
    m_ref[idx, :, lanes] = m_new


def _run_pipelined(units, ready, following):
    pending = list(ready)
    stream = [scores for scores, _ in units] + [scores for scores, _ in following]
    for n, (_, update) in enumerate(units):
        if n + SCORE_LOOKAHEAD < len(stream):
            pending.append(stream[n + SCORE_LOOKAHEAD]())
        update(pending.pop(0))
    return pending


def _attn_kernel(qat_ref, dq1t_ref, dq2t_ref, ga_ref, gb_ref, x_ref,
                 ka_ref, vat_ref, dk_ref, dvt_ref, wout_ref,
                 lq1_ref, lk1_ref, lq2_ref, lk2_ref, subg_ref, postg_ref,
                 out_ref, m_ref, acc_ref, carry_ref, *, lambda_init):
    qi = pl.program_id(1)
    m_ref[...] = jnp.full(m_ref.shape, -jnp.inf, F32)
    acc_ref[...] = jnp.zeros(acc_ref.shape, F32)

    ones = jnp.ones((BF16_ROWS, KV_TILE), BF16)
    key = lax.broadcasted_iota(jnp.int32, (KV_TILE, Q_TILE), 0)
    qry = lax.broadcasted_iota(jnp.int32, (KV_TILE, Q_TILE), 1)
    causal = key <= qry
    all_q = slice(0, Q_TILE)
    upper_q = slice(KV_TILE, Q_TILE)

    def tile_units(j, lanes, mask):
        rows = pl.ds(pl.multiple_of(j * KV_TILE, KV_TILE), KV_TILE)
        units = []
        for u in range(N_MAPS):
            if u < MLA_HEADS:
                cs = slice(u * MLA_QK_PAD, (u + 1) * MLA_QK_PAD)
                vs = slice(u * MLA_V, (u + 1) * MLA_V)
                k_ref, qt_ref, v_ref = ka_ref, qat_ref, vat_ref
            else:
                d = u - MLA_HEADS
                cs = vs = slice((d // 2) * LANES, (d // 2 + 1) * LANES)
                k_ref, qt_ref, v_ref = dk_ref, (dq2t_ref if d % 2 else dq1t_ref), dvt_ref

            def scores(cs=cs, k_ref=k_ref, qt_ref=qt_ref):
                return jnp.dot(k_ref[rows, cs], qt_ref[cs, lanes], preferred_element_type=F32)

            def update(s_t, u=u, vs=vs, v_ref=v_ref):
                if mask is not None:
                    s_t = jnp.where(mask, s_t, -jnp.inf)
                vt = jnp.concatenate([v_ref[j, vs, :], ones], axis=0)
                _softmax_pv(s_t, vt, m_ref, acc_ref, u, lanes)

            units.append((scores, update))
        return units

    def load_carry():
        return [carry_ref[n] for n in range(SCORE_LOOKAHEAD)]

    def store_carry(scores):
        for n, s_t in enumerate(scores):
            carry_ref[n] = s_t

    store_carry([scores() for scores, _ in tile_units(0, all_q, None)[:SCORE_LOOKAHEAD]])

    def body(i, carry):
        units = tile_units(2 * i, all_q, None) + tile_units(2 * i + 1, all_q, None)
        store_carry(_run_pipelined(units, load_carry(), tile_units(2 * i + 2, all_q, None)))
        return carry

    lax.fori_loop(0, qi, body, 0)
    _run_pipelined(tile_units(2 * qi, all_q, causal)
                   + tile_units(2 * qi + 1, upper_q, causal[:, :KV_TILE]), load_carry(), [])

    def normalised(u):
        acc = acc_ref[u]
        return acc[:MLA_V] * (1.0 / acc[MLA_V:MLA_V + 1])

    oa = jnp.concatenate([normalised(h).T for h in range(MLA_HEADS)], axis=1)
    oa = oa * ga_ref[...].astype(F32)
    lam = (jnp.exp(jnp.sum(lq1_ref[...] * lk1_ref[...], axis=1, keepdims=True))
           - jnp.exp(jnp.sum(lq2_ref[...] * lk2_ref[...], axis=1, keepdims=True))
           + lambda_init)
    ob = []
    for h in range(DIFF_HEADS):
        o_t = normalised(MLA_HEADS + 2 * h) - lam * normalised(MLA_HEADS + 2 * h + 1)
        o_t = o_t * lax.rsqrt(jnp.mean(o_t * o_t, axis=0, keepdims=True) + SUBLN_EPS)
        ob.append(o_t.T * subg_ref[...] * (1.0 - lambda_init))
    ob = jnp.concatenate(ob, axis=1) * gb_ref[...].astype(F32)

    mixed_in = jnp.concatenate([oa, ob], axis=1).astype(BF16)
    mixed = jnp.dot(mixed_in, wout_ref[...], preferred_element_type=F32)
    y = mixed * lax.rsqrt(jnp.mean(mixed * mixed, axis=-1, keepdims=True) + NORM_EPS)
    out_ref[...] = x_ref[...] + y * postg_ref[...]


def _layer(x3, ln_pre_g, w_in, kv_norm_g, w_uk, w_uv, lq1, lk1, lq2, lk2,
           subln_g, w_out, ln_post_g, tables, lambda_init):
    batch, seq, d_model = x3.shape
    w_tok, w_feat = _split_weights(w_in)
    g2 = ln_pre_g.reshape(1, d_model)
    kvg2 = kv_norm_g.reshape(1, KV_RANK)
    wuk = w_uk.astype(BF16)
    wuvt = w_uv.T.astype(BF16)

    seq_tiles = seq // PROJ_TILE
    kv_per_tile = PROJ_TILE // KV_TILE
    n_kv = seq // KV_TILE
    tok_blk = lambda w: pl.BlockSpec((None, PROJ_TILE, w), lambda b, t: (b, t, 0))
    feat_blk = lambda r: pl.BlockSpec((None, r, PROJ_TILE), lambda b, t: (b, 0, t))
    vt_blk = pl.BlockSpec((None, kv_per_tile, 512, KV_TILE), lambda b, t: (b, t, 0, 0))
    full = lambda a: pl.BlockSpec(a.shape, lambda b, t: (0,) * a.ndim)
    tab_tok = pl.BlockSpec((PROJ_TILE, LANES), lambda b, t: (t, 0))
    tab_feat = lambda a: pl.BlockSpec((a.shape[0], PROJ_TILE), lambda b, t: (0, t))
    tok_tabs, feat_tabs = tables[:6], tables[6:]
    sds = jax.ShapeDtypeStruct
    qat, dq1t, dq2t, vat, dvt, ka, dk, ga, gb = pl.pallas_call(
        _proj_kernel,
        grid=(batch, seq_tiles),
        in_specs=[tok_blk(d_model), full(g2), full(w_tok), full(w_feat), full(kvg2),
                  full(wuk), full(wuvt)] + [tab_tok] * 6 + [tab_feat(a) for a in feat_tabs],
        out_specs=[feat_blk(1024), feat_blk(512), feat_blk(512), vt_blk, vt_blk,
                   tok_blk(1024), tok_blk(512), tok_blk(512), tok_blk(512)],
        out_shape=[sds((batch, 1024, seq), BF16), sds((batch, 512, seq), BF16),
                   sds((batch, 512, seq), BF16),
                   sds((batch, n_kv, 512, KV_TILE), BF16), sds((batch, n_kv, 512, KV_TILE), BF16),
                   sds((batch, seq, 1024), BF16), sds((batch, seq, 512), BF16),
                   sds((batch, seq, 512), BF16), sds((batch, seq, 512), BF16)],
        compiler_params=pltpu.CompilerParams(
            dimension_semantics=("arbitrary", "arbitrary"), vmem_limit_bytes=VMEM_LIMIT_BYTES),
        name="proj",
    )(x3, g2, w_tok, w_feat, kvg2, wuk, wuvt, *tok_tabs, *feat_tabs)

    q_tok = lambda w: pl.BlockSpec((None, Q_TILE, w), lambda b, i: (b, i, 0))
    q_feat = lambda r: pl.BlockSpec((None, r, Q_TILE), lambda b, i: (b, 0, i))
    kv_tok = lambda w: pl.BlockSpec((None, seq, w), lambda b, i: (b, 0, 0))
    kv_feat = pl.BlockSpec((None, n_kv, 512, KV_TILE), lambda b, i: (b, 0, 0, 0))
    fullb = lambda a: pl.BlockSpec(a.shape, lambda b, i: (0,) * a.ndim)
    wout = w_out.astype(BF16)
    lvecs = [v.reshape(1, DIFF_QK) for v in (lq1, lk1, lq2, lk2)]
    subg2 = subln_g.reshape(1, DIFF_V)
    postg2 = ln_post_g.reshape(1, d_model)
    return pl.pallas_call(
        functools.partial(_attn_kernel, lambda_init=lambda_init),
        grid=(batch, seq // Q_TILE),
        in_specs=[q_feat(1024), q_feat(512), q_feat(512), q_tok(512), q_tok(512), q_tok(d_model),
                  kv_tok(1024), kv_feat, kv_tok(512), kv_feat, fullb(wout)]
                 + [fullb(v) for v in lvecs] + [fullb(subg2), fullb(postg2)],
        out_specs=q_tok(d_model),
        out_shape=sds((batch, seq, d_model), F32),
        scratch_shapes=[
            pltpu.VMEM((N_MAPS, 1, Q_TILE), F32),
            pltpu.VMEM((N_MAPS, ACC_ROWS, Q_TILE), F32),
            pltpu.VMEM((SCORE_LOOKAHEAD, KV_TILE, Q_TILE), F32),
        ],
        compiler_params=pltpu.CompilerParams(
            dimension_semantics=("arbitrary", "arbitrary"), vmem_limit_bytes=VMEM_LIMIT_BYTES),
        name="attn",
    )(qat, dq1t, dq2t, ga, gb, x3, ka, vat, dk, dvt, wout, *lvecs, subg2, postg2)


def kernel(x, ln_pre_g, w_in, kv_norm_g, w_uk, w_uv, lambda_q1, lambda_k1,
           lambda_q2, lambda_k2, subln_g, w_out, ln_post_g):
    batch, seq, d_model = x.shape
    depth = w_in.shape[0]
    assert seq % PROJ_TILE == 0 and PROJ_TILE % KV_TILE == 0 and seq % Q_TILE == 0
    assert w_in.shape[2] == _C_END and d_model == 1024
    tables = _rope_tables(seq)
    for l in range(depth):
        lambda_init = 0.8 - 0.6 * math.exp(-0.3 * l)
        x = _layer(x, ln_pre_g[l], w_in[l], kv_norm_g[l], w_uk[l], w_uv[l],
                   lambda_q1[l], lambda_k1[l], lambda_q2[l], lambda_k2[l], subln_g[l],
                   w_out[l], ln_post_g[l], tables, lambda_init)
    return x
```

```python
import functools
import math

import jax
import jax.numpy as jnp
import numpy as np
from jax import lax
from jax.experimental import pallas as pl
from jax.experimental.pallas import tpu as pltpu

ROPE_THETA = 500000.0
NORM_EPS = 1e-6
SUBLN_EPS = 1e-5

MLA_HEADS = 4
MLA_NOPE = 128
MLA_ROPE = 64
MLA_V = 128
KV_RANK = 128
DIFF_HEADS = 4
DIFF_QK = 64
DIFF_V = 2 * DIFF_QK
DIFF_ROPE = DIFF_QK // 4

LANES = 128
BF16_ROWS = 16
MLA_QK_PAD = 2 * LANES
LOG2E = math.log2(math.e)
MLA_QSCALE = LOG2E / math.sqrt(MLA_NOPE + MLA_ROPE)
DIFF_QSCALE = LOG2E / math.sqrt(DIFF_QK)

PROJ_TILE = 512
KV_TILE = 256
Q_TILE = 2 * KV_TILE
N_MAPS = MLA_HEADS + 2 * DIFF_HEADS
ACC_ROWS = MLA_V + BF16_ROWS
SCORE_LOOKAHEAD = 2
VMEM_LIMIT_BYTES = 56 * 1024 * 1024

BF16 = jnp.bfloat16
F32 = jnp.float32
_NT = (((1,), (1,)), ((), ()))

_C_QN = 0
_C_QR = _C_QN + MLA_HEADS * MLA_NOPE
_C_CKV = _C_QR + MLA_HEADS * MLA_ROPE
_C_KR = _C_CKV + KV_RANK
_C_GA = _C_KR + MLA_ROPE
_C_DQ = _C_GA + MLA_HEADS * MLA_V
_C_DK = _C_DQ + DIFF_HEADS * 2 * DIFF_QK
_C_DV = _C_DK + DIFF_HEADS * 2 * DIFF_QK
_C_GB = _C_DV + DIFF_HEADS * DIFF_V
_C_END = _C_GB + DIFF_HEADS * DIFF_V
_T_CKV, _T_KR, _T_GA, _T_DK, _T_GB = 0, 128, 256, 768, 1280
_F_QN, _F_QR, _F_DQ, _F_DV = 0, 512, 768, 1280


def _split_weights(w_in):
    pad = jnp.zeros((w_in.shape[0], LANES - MLA_ROPE), w_in.dtype)
    w_tok = jnp.concatenate([w_in[:, _C_CKV:_C_GA], pad, w_in[:, _C_GA:_C_DQ],
                             w_in[:, _C_DK:_C_DV], w_in[:, _C_GB:_C_END]], axis=1)
    w_feat = jnp.concatenate([w_in[:, _C_QN:_C_CKV], w_in[:, _C_DQ:_C_DK],
                              w_in[:, _C_DV:_C_GB]], axis=1).T
    return w_tok.astype(BF16), w_feat.astype(BF16)


def _rope_tables(seq):
    def tab(dim):
        inv_freq = ROPE_THETA ** (-np.arange(0, dim, 2, dtype=np.float64) / dim)
        ang = np.arange(seq, dtype=np.float64)[:, None] * inv_freq[None, :]
        return np.cos(ang), np.sin(ang)

    def lane_tables(c, s, group, rest_cos):
        half = c.shape[1]
        rest = np.full((seq, group - 2 * half), rest_cos)
        zero = np.zeros((seq, group - 2 * half))
        zh = np.zeros((seq, half))
        reps = LANES // group
        return (np.concatenate([c, c, rest] * reps, axis=1),
                np.concatenate([-s, zh, zero] * reps, axis=1),
                np.concatenate([zh, s, zero] * reps, axis=1))

    ca, sa = tab(MLA_ROPE)
    cb, sb = tab(DIFF_ROPE)
    tables = lane_tables(ca, sa, LANES, 0.0)
    tables += lane_tables(cb, sb, DIFF_QK, 1.0)
    tables += (ca.T, sa.T, cb.T, sb.T)
    return tuple(jnp.asarray(t, F32) for t in tables)


def _silu(g):
    return g / (1.0 + jnp.exp(-g))


def _rope_lanes(v, cos, sin_up, sin_down, half):
    return (v * cos + pltpu.roll(v, LANES - half, axis=1) * sin_up
            + pltpu.roll(v, half, axis=1) * sin_down)


def _proj_kernel(x_ref, g_ref, wtok_ref, wfeat_ref, kvg_ref, wuk_ref, wuvt_ref,
                 ca_ref, sau_ref, sad_ref, cb_ref, sbu_ref, sbd_ref,
                 cat_ref, sat_ref, cbt_ref, sbt_ref,
                 qat_ref, dq1t_ref, dq2t_ref, vat_ref, dvt_ref, ka_ref, dk_ref, ga_ref, gb_ref):
    x = x_ref[...]
    ms = jnp.mean(x * x, axis=-1, keepdims=True)
    h = (x * lax.rsqrt(ms + NORM_EPS) * g_ref[...]).astype(BF16)

    tok = jnp.dot(h, wtok_ref[...], preferred_element_type=F32)
    feat = lax.dot_general(wfeat_ref[...], h, _NT, preferred_element_type=F32)

    cat, sat = cat_ref[...], sat_ref[...]
    half = MLA_ROPE // 2
    zpad = jnp.zeros((MLA_QK_PAD - MLA_NOPE - MLA_ROPE, PROJ_TILE), BF16)
    for hd in range(MLA_HEADS):
        base = hd * MLA_QK_PAD
        qn = feat[_F_QN + hd * MLA_NOPE:_F_QN + (hd + 1) * MLA_NOPE]
        qat_ref[base:base + MLA_NOPE, :] = (qn * MLA_QSCALE).astype(BF16)
        x1 = feat[_F_QR + hd * MLA_ROPE:_F_QR + hd * MLA_ROPE + half]
        x2 = feat[_F_QR + hd * MLA_ROPE + half:_F_QR + (hd + 1) * MLA_ROPE]
        r1 = (x1 * cat - x2 * sat) * MLA_QSCALE
        r2 = (x2 * cat + x1 * sat) * MLA_QSCALE
        rope_rows = jnp.concatenate([r1.astype(BF16), r2.astype(BF16), zpad], axis=0)
        qat_ref[base + MLA_NOPE:base + MLA_QK_PAD, :] = rope_rows

    cbt, sbt = cbt_ref[...], sbt_ref[...]
    rh = DIFF_ROPE // 2
    zmap = jnp.zeros((DIFF_QK, PROJ_TILE), BF16)
    for hd in range(DIFF_HEADS):
        maps = []
        for mp in range(2):
            d = feat[_F_DQ + (2 * hd + mp) * DIFF_QK:_F_DQ + (2 * hd + mp + 1) * DIFF_QK]
            x1, x2 = d[:rh], d[rh:2 * rh]
            q = jnp.concatenate([x1 * cbt - x2 * sbt, x2 * cbt + x1 * sbt, d[2 * rh:]], axis=0)
            maps.append((q * DIFF_QSCALE).astype(BF16))
        sl = slice(hd * LANES, (hd + 1) * LANES)
        dq1t_ref[sl, :] = jnp.concatenate([maps[0], zmap], axis=0)
        dq2t_ref[sl, :] = jnp.concatenate([zmap, maps[1]], axis=0)
    for c in range(PROJ_TILE // KV_TILE):
        dvt_ref[c] = feat[_F_DV:, c * KV_TILE:(c + 1) * KV_TILE].astype(BF16)

    c = tok[:, _T_CKV:_T_CKV + KV_RANK]
    c = c * lax.rsqrt(jnp.mean(c * c, axis=-1, keepdims=True) + NORM_EPS) * kvg_ref[...]
    c = c.astype(BF16)
    kn = jnp.dot(c, wuk_ref[...], preferred_element_type=F32)
    vt = lax.dot_general(wuvt_ref[...], c, _NT, preferred_element_type=F32)
    for ci in range(PROJ_TILE // KV_TILE):
        vat_ref[ci] = vt[:, ci * KV_TILE:(ci + 1) * KV_TILE].astype(BF16)
    kr = _rope_lanes(tok[:, _T_KR:_T_KR + LANES], ca_ref[...], sau_ref[...], sad_ref[...], half)
    kr = kr.astype(BF16)
    for hd in range(MLA_HEADS):
        base = hd * MLA_QK_PAD
        ka_ref[:, base:base + LANES] = kn[:, hd * LANES:(hd + 1) * LANES].astype(BF16)
        ka_ref[:, base + LANES:base + 2 * LANES] = kr
    ga_ref[...] = _silu(tok[:, _T_GA:_T_GA + 512]).astype(BF16)
    cb, sbu, sbd = cb_ref[...], sbu_ref[...], sbd_ref[...]
    for hd in range(DIFF_HEADS):
        sl = slice(_T_DK + hd * LANES, _T_DK + (hd + 1) * LANES)
        dk_ref[:, hd * LANES:(hd + 1) * LANES] = _rope_lanes(tok[:, sl], cb, sbu, sbd, rh).astype(BF16)
    gb_ref[...] = _silu(tok[:, _T_GB:_T_GB + 512]).astype(BF16)


def _softmax_pv(s_t, vt, m_ref, acc_ref, idx, lanes):
    m_old = m_ref[idx, :, lanes]
    m_new = jnp.maximum(m_old, jnp.max(s_t, axis=0, keepdims=True))
    alpha = jnp.exp2(m_old - m_new)
    p_t = jnp.exp2(s_t - m_new).astype(BF16)
    pv = jnp.dot(vt, p_t, preferred_element_type=F32)
    acc_ref[idx, :, lanes] = alpha * acc_ref[idx, :, lanes] + pv
    m_ref[idx, :, lanes] = m_new


def _run_pipelined(units, ready, following):
    pending = list(ready)
    stream = [scores for scores, _ in units] + [scores for scores, _ in following]
    for n, (_, update) in enumerate(units):
        if n + SCORE_LOOKAHEAD < len(stream):
            pending.append(stream[n + SCORE_LOOKAHEAD]())
        update(pending.pop(0))
    return pending


def _attn_kernel(qat_ref, dq1t_ref, dq2t_ref, ga_ref, gb_ref, x_ref,
                 ka_ref, vat_ref, dk_ref, dvt_ref, wout_ref,
                 lq1_ref, lk1_ref, lq2_ref, lk2_ref, subg_ref, postg_ref,
                 out_ref, m_ref, acc_ref, carry_ref, *, lambda_init):
    qi = pl.program_id(1)

    def reset_state():
        m_ref[...] = jnp.full(m_ref.shape, -jnp.inf, F32)
        acc_ref[...] = jnp.zeros(acc_ref.shape, F32)

    pl.when((pl.program_id(0) == 0) & (qi == 0))(reset_state)

    ones = jnp.ones((BF16_ROWS, KV_TILE), BF16)
    key = lax.broadcasted_iota(jnp.int32, (KV_TILE, Q_TILE), 0)
    qry = lax.broadcasted_iota(jnp.int32, (KV_TILE, Q_TILE), 1)
    causal = key <= qry
    all_q = slice(0, Q_TILE)
    upper_q = slice(KV_TILE, Q_TILE)

    def tile_units(j, lanes, mask):
        rows = pl.ds(pl.multiple_of(j * KV_TILE, KV_TILE), KV_TILE)
        units = []
        for u in range(N_MAPS):
            if u < MLA_HEADS:
                cs = slice(u * MLA_QK_PAD, (u + 1) * MLA_QK_PAD)
                vs = slice(u * MLA_V, (u + 1) * MLA_V)
                k_ref, qt_ref, v_ref = ka_ref, qat_ref, vat_ref
            else:
                d = u - MLA_HEADS
                cs = vs = slice((d // 2) * LANES, (d // 2 + 1) * LANES)
                k_ref, qt_ref, v_ref = dk_ref, (dq2t_ref if d % 2 else dq1t_ref), dvt_ref

            def scores(cs=cs, k_ref=k_ref, qt_ref=qt_ref):
                return jnp.dot(k_ref[rows, cs], qt_ref[cs, lanes], preferred_element_type=F32)

            def update(s_t, u=u, vs=vs, v_ref=v_ref):
                if mask is not None:
                    s_t = jnp.where(mask, s_t, -jnp.inf)
                vt = jnp.concatenate([v_ref[j, vs, :], ones], axis=0)
                _softmax_pv(s_t, vt, m_ref, acc_ref, u, lanes)

            units.append((scores, update))
        return units

    def load_carry():
        return [carry_ref[n] for n in range(SCORE_LOOKAHEAD)]

    def store_carry(scores):
        for n, s_t in enumerate(scores):
            carry_ref[n] = s_t

    store_carry([scores() for scores, _ in tile_units(0, all_q, None)[:SCORE_LOOKAHEAD]])

    def body(i, carry):
        units = tile_units(2 * i, all_q, None) + tile_units(2 * i + 1, all_q, None)
        store_carry(_run_pipelined(units, load_carry(), tile_units(2 * i + 2, all_q, None)))
        return carry

    lax.fori_loop(0, qi, body, 0)
    _run_pipelined(tile_units(2 * qi, all_q, causal)
                   + tile_units(2 * qi + 1, upper_q, causal[:, :KV_TILE]), load_carry(), [])

    lam = (jnp.exp(jnp.sum(lq1_ref[...] * lk1_ref[...], axis=1, keepdims=True))
           - jnp.exp(jnp.sum(lq2_ref[...] * lk2_ref[...], axis=1, keepdims=True))
           + lambda_init)

    def scaled(u, factor):
        acc = acc_ref[u]
        return acc[:MLA_V] * (factor / acc[MLA_V:MLA_V + 1])

    oa = jnp.concatenate([scaled(h, 1.0).T for h in range(MLA_HEADS)], axis=1)
    oa = oa * ga_ref[...].astype(F32)
    sub_gain = subg_ref[...] * (1.0 - lambda_init)
    ob = []
    for h in range(DIFF_HEADS):
        o_t = scaled(MLA_HEADS + 2 * h, 1.0) - scaled(MLA_HEADS + 2 * h + 1, lam)
        o_t = o_t * lax.rsqrt(jnp.mean(o_t * o_t, axis=0, keepdims=True) + SUBLN_EPS)
        ob.append(o_t.T * sub_gain)
    ob = jnp.concatenate(ob, axis=1) * gb_ref[...].astype(F32)

    mixed_in = jnp.concatenate([oa, ob], axis=1).astype(BF16)
    reset_state()
    mixed = jnp.dot(mixed_in, wout_ref[...], preferred_element_type=F32)
    y = mixed * lax.rsqrt(jnp.mean(mixed * mixed, axis=-1, keepdims=True) + NORM_EPS)
    out_ref[...] = x_ref[...] + y * postg_ref[...]


def _layer(x3, ln_pre_g, w_in, kv_norm_g, w_uk, w_uv, lq1, lk1, lq2, lk2,
           subln_g, w_out, ln_post_g, tables, lambda_init):
    batch, seq, d_model = x3.shape
    w_tok, w_feat = _split_weights(w_in)
    g2 = ln_pre_g.reshape(1, d_model)
    kvg2 = kv_norm_g.reshape(1, KV_RANK)
    wuk = w_uk.astype(BF16)
    wuvt = w_uv.T.astype(BF16)

    seq_tiles = seq // PROJ_TILE
    kv_per_tile = PROJ_TILE // KV_TILE
    n_kv = seq // KV_TILE
    tok_blk = lambda w: pl.BlockSpec((None, PROJ_TILE, w), lambda b, t: (b, t, 0))
    feat_blk = lambda r: pl.BlockSpec((None, r, PROJ_TILE), lambda b, t: (b, 0, t))
    vt_blk = pl.BlockSpec((None, kv_per_tile, 512, KV_TILE), lambda b, t: (b, t, 0, 0))
    full = lambda a: pl.BlockSpec(a.shape, lambda b, t: (0,) * a.ndim)
    tab_tok = pl.BlockSpec((PROJ_TILE, LANES), lambda b, t: (t, 0))
    tab_feat = lambda a: pl.BlockSpec((a.shape[0], PROJ_TILE), lambda b, t: (0, t))
    tok_tabs, feat_tabs = tables[:6], tables[6:]
    sds = jax.ShapeDtypeStruct
    qat, dq1t, dq2t, vat, dvt, ka, dk, ga, gb = pl.pallas_call(
        _proj_kernel,
        grid=(batch, seq_tiles),
        in_specs=[tok_blk(d_model), full(g2), full(w_tok), full(w_feat), full(kvg2),
                  full(wuk), full(wuvt)] + [tab_tok] * 6 + [tab_feat(a) for a in feat_tabs],
        out_specs=[feat_blk(1024), feat_blk(512), feat_blk(512), vt_blk, vt_blk,
                   tok_blk(1024), tok_blk(512), tok_blk(512), tok_blk(512)],
        out_shape=[sds((batch, 1024, seq), BF16), sds((batch, 512, seq), BF16),
                   sds((batch, 512, seq), BF16),
                   sds((batch, n_kv, 512, KV_TILE), BF16), sds((batch, n_kv, 512, KV_TILE), BF16),
                   sds((batch, seq, 1024), BF16), sds((batch, seq, 512), BF16),
                   sds((batch, seq, 512), BF16), sds((batch, seq, 512), BF16)],
        compiler_params=pltpu.CompilerParams(
            dimension_semantics=("arbitrary", "arbitrary"), vmem_limit_bytes=VMEM_LIMIT_BYTES),
        name="proj",
    )(x3, g2, w_tok, w_feat, kvg2, wuk, wuvt, *tok_tabs, *feat_tabs)

    q_tok = lambda w: pl.BlockSpec((None, Q_TILE, w), lambda b, i: (b, i, 0))
    q_feat = lambda r: pl.BlockSpec((None, r, Q_TILE), lambda b, i: (b, 0, i))
    kv_tok = lambda w: pl.BlockSpec((None, seq, w), lambda b, i: (b, 0, 0))
    kv_feat = pl.BlockSpec((None, n_kv, 512, KV_TILE), lambda b, i: (b, 0, 0, 0))
    fullb = lambda a: pl.BlockSpec(a.shape, lambda b, i: (0,) * a.ndim)
    wout = w_out.astype(BF16)
    lvecs = [v.reshape(1, DIFF_QK) for v in (lq1, lk1, lq2, lk2)]
    subg2 = subln_g.reshape(1, DIFF_V)
    postg2 = ln_post_g.reshape(1, d_model)
    return pl.pallas_call(
        functools.partial(_attn_kernel, lambda_init=lambda_init),
        grid=(batch, seq // Q_TILE),
        in_specs=[q_feat(1024), q_feat(512), q_feat(512), q_tok(512), q_tok(512), q_tok(d_model),
                  kv_tok(1024), kv_feat, kv_tok(512), kv_feat, fullb(wout)]
                 + [fullb(v) for v in lvecs] + [fullb(subg2), fullb(postg2)],
        out_specs=q_tok(d_model),
        out_shape=sds((batch, seq, d_model), F32),
        scratch_shapes=[
            pltpu.VMEM((N_MAPS, 1, Q_TILE), F32),
            pltpu.VMEM((N_MAPS, ACC_ROWS, Q_TILE), F32),
            pltpu.VMEM((SCORE_LOOKAHEAD, KV_TILE, Q_TILE), F32),
        ],
        compiler_params=pltpu.CompilerParams(
            dimension_semantics=("arbitrary", "arbitrary"), vmem_limit_bytes=VMEM_LIMIT_BYTES),
        name="attn",
    )(qat, dq1t, dq2t, ga, gb, x3, ka, vat, dk, dvt, wout, *lvecs, subg2, postg2)


def kernel(x, ln_pre_g, w_in, kv_norm_g, w_uk, w_uv, lambda_q1, lambda_k1,
           lambda_q2, lambda_k2, subln_g, w_out, ln_post_g):
    batch, seq, d_model = x.shape
    depth = w_in.shape[0]
    assert seq % PROJ_TILE == 0 and PROJ_TILE % KV_TILE == 0 and seq % Q_TILE == 0
    assert w_in.shape[2] == _C_END and d_model == 1024
    tables = _rope_tables(seq)
    for l in range(depth):
        lambda_init = 0.8 - 0.6 * math.exp(-0.3 * l)
        x = _layer(x, ln_pre_g[l], w_in[l], kv_norm_g[l], w_uk[l], w_uv[l],
                   lambda_q1[l], lambda_k1[l], lambda_q2[l], lambda_k2[l], subln_g[l],
                   w_out[l], ln_post_g[l], tables, lambda_init)
    return x
```

```python
import functools
import math

import jax
import jax.numpy as jnp
import numpy as np
from jax import lax
from jax.experimental import pallas as pl
from jax.experimental.pallas import tpu as pltpu

ROPE_THETA = 500000.0
NORM_EPS = 1e-6
SUBLN_EPS = 1e-5

MLA_HEADS = 4
MLA_NOPE = 128
MLA_ROPE = 64
MLA_V = 128
KV_RANK = 128
DIFF_HEADS = 4
DIFF_QK = 64
DIFF_V = 2 * DIFF_QK
DIFF_ROPE = DIFF_QK // 4

LANES = 128
BF16_ROWS = 16
MLA_QK_PAD = 2 * LANES
LOG2E = math.log2(math.e)
MLA_QSCALE = LOG2E / math.sqrt(MLA_NOPE + MLA_ROPE)
DIFF_QSCALE = LOG2E / math.sqrt(DIFF_QK)

PROJ_TILE = 512
KV_TILE = 256
Q_TILE = 2 * KV_TILE
N_MAPS = MLA_HEADS + 2 * DIFF_HEADS
ACC_ROWS = MLA_V + BF16_ROWS
SCORE_LOOKAHEAD = 2
VMEM_LIMIT_BYTES = 56 * 1024 * 1024

BF16 = jnp.bfloat16
F32 = jnp.float32
_NT = (((1,), (1,)), ((), ()))

_C_QN = 0
_C_QR = _C_QN + MLA_HEADS * MLA_NOPE
_C_CKV = _C_QR + MLA_HEADS * MLA_ROPE
_C_KR = _C_CKV + KV_RANK
_C_GA = _C_KR + MLA_ROPE
_C_DQ = _C_GA + MLA_HEADS * MLA_V
_C_DK = _C_DQ + DIFF_HEADS * 2 * DIFF_QK
_C_DV = _C_DK + DIFF_HEADS * 2 * DIFF_QK
_C_GB = _C_DV + DIFF_HEADS * DIFF_V
_C_END = _C_GB + DIFF_HEADS * DIFF_V
_T_CKV, _T_KR, _T_GA, _T_DK, _T_GB = 0, 128, 256, 768, 1280
_F_QN, _F_QR, _F_DQ, _F_DV = 0, 512, 768, 1280


def _split_weights(w_in):
    pad = jnp.zeros((w_in.shape[0], LANES - MLA_ROPE), w_in.dtype)
    w_tok = jnp.concatenate([w_in[:, _C_CKV:_C_GA], pad, w_in[:, _C_GA:_C_DQ],
                             w_in[:, _C_DK:_C_DV], w_in[:, _C_GB:_C_END]], axis=1)
    w_feat = jnp.concatenate([w_in[:, _C_QN:_C_CKV], w_in[:, _C_DQ:_C_DK],
                              w_in[:, _C_DV:_C_GB]], axis=1).T
    return w_tok.astype(BF16), w_feat.astype(BF16)


def _rope_tables(seq):
    def tab(dim):
        inv_freq = ROPE_THETA ** (-np.arange(0, dim, 2, dtype=np.float64) / dim)
        ang = np.arange(seq, dtype=np.float64)[:, None] * inv_freq[None, :]
        return np.cos(ang), np.sin(ang)

    def lane_tables(c, s, group, rest_cos):
        half = c.shape[1]
        rest = np.full((seq, group - 2 * half), rest_cos)
        zero = np.zeros((seq, group - 2 * half))
        zh = np.zeros((seq, half))
        reps = LANES // group
        return (np.concatenate([c, c, rest] * reps, axis=1),
                np.concatenate([-s, zh, zero] * reps, axis=1),
                np.concatenate([zh, s, zero] * reps, axis=1))

    ca, sa = tab(MLA_ROPE)
    cb, sb = tab(DIFF_ROPE)
    tables = lane_tables(ca, sa, LANES, 0.0)
    tables += lane_tables(cb, sb, DIFF_QK, 1.0)
    tables += (ca.T, sa.T, cb.T, sb.T)
    return tuple(jnp.asarray(t, F32) for t in tables)


def _silu(g):
    return g / (1.0 + jnp.exp(-g))


def _rope_lanes(v, cos, sin_up, sin_down, half):
    return (v * cos + pltpu.roll(v, LANES - half, axis=1) * sin_up
            + pltpu.roll(v, half, axis=1) * sin_down)


def _proj_kernel(x_ref, g_ref, wtok_ref, wfeat_ref, kvg_ref, wuk_ref, wuvt_ref,
                 ca_ref, sau_ref, sad_ref, cb_ref, sbu_ref, sbd_ref,
                 cat_ref, sat_ref, cbt_ref, sbt_ref,
                 qat_ref, dq1t_ref, dq2t_ref, vat_ref, dvt_ref, ka_ref, dk_ref, ga_ref, gb_ref):
    x = x_ref[...]
    ms = jnp.mean(x * x, axis=-1, keepdims=True)
    h = (x * lax.rsqrt(ms + NORM_EPS) * g_ref[...]).astype(BF16)

    tok = jnp.dot(h, wtok_ref[...], preferred_element_type=F32)
    feat = lax.dot_general(wfeat_ref[...], h, _NT, preferred_element_type=F32)

    cat, sat = cat_ref[...], sat_ref[...]
    half = MLA_ROPE // 2
    zpad = jnp.zeros((MLA_QK_PAD - MLA_NOPE - MLA_ROPE, PROJ_TILE), BF16)
    for hd in range(MLA_HEADS):
        base = hd * MLA_QK_PAD
        qn = feat[_F_QN + hd * MLA_NOPE:_F_QN + (hd + 1) * MLA_NOPE]
        qat_ref[base:base + MLA_NOPE, :] = (qn * MLA_QSCALE).astype(BF16)
        x1 = feat[_F_QR + hd * MLA_ROPE:_F_QR + hd * MLA_ROPE + half]
        x2 = feat[_F_QR + hd * MLA_ROPE + half:_F_QR + (hd + 1) * MLA_ROPE]
        r1 = (x1 * cat - x2 * sat) * MLA_QSCALE
        r2 = (x2 * cat + x1 * sat) * MLA_QSCALE
        rope_rows = jnp.concatenate([r1.astype(BF16), r2.astype(BF16), zpad], axis=0)
        qat_ref[base + MLA_NOPE:base + MLA_QK_PAD, :] = rope_rows

    cbt, sbt = cbt_ref[...], sbt_ref[...]
    rh = DIFF_ROPE // 2
    zmap = jnp.zeros((DIFF_QK, PROJ_TILE), BF16)
    for hd in range(DIFF_HEADS):
        maps = []
        for mp in range(2):
            d = feat[_F_DQ + (2 * hd + mp) * DIFF_QK:_F_DQ + (2 * hd + mp + 1) * DIFF_QK]
            x1, x2 = d[:rh], d[rh:2 * rh]
            q = jnp.concatenate([x1 * cbt - x2 * sbt, x2 * cbt + x1 * sbt, d[2 * rh:]], axis=0)
            maps.append((q * DIFF_QSCALE).astype(BF16))
        sl = slice(hd * LANES, (hd + 1) * LANES)
        dq1t_ref[sl, :] = jnp.concatenate([maps[0], zmap], axis=0)
        dq2t_ref[sl, :] = jnp.concatenate([zmap, maps[1]], axis=0)
    for c in range(PROJ_TILE // KV_TILE):
        dvt_ref[c] = feat[_F_DV:, c * KV_TILE:(c + 1) * KV_TILE].astype(BF16)

    c = tok[:, _T_CKV:_T_CKV + KV_RANK]
    c = c * lax.rsqrt(jnp.mean(c * c, axis=-1, keepdims=True) + NORM_EPS) * kvg_ref[...]
    c = c.astype(BF16)
    kn = jnp.dot(c, wuk_ref[...], preferred_element_type=F32)
    vt = lax.dot_general(wuvt_ref[...], c, _NT, preferred_element_type=F32)
    for ci in range(PROJ_TILE // KV_TILE):
        vat_ref[ci] = vt[:, ci * KV_TILE:(ci + 1) * KV_TILE].astype(BF16)
    kr = _rope_lanes(tok[:, _T_KR:_T_KR + LANES], ca_ref[...], sau_ref[...], sad_ref[...], half)
    kr = kr.astype(BF16)
    for hd in range(MLA_HEADS):
        base = hd * MLA_QK_PAD
        ka_ref[:, base:base + LANES] = kn[:, hd * LANES:(hd + 1) * LANES].astype(BF16)
        ka_ref[:, base + LANES:base + 2 * LANES] = kr
    ga_ref[...] = _silu(tok[:, _T_GA:_T_GA + 512]).astype(BF16)
    cb, sbu, sbd = cb_ref[...], sbu_ref[...], sbd_ref[...]
    for hd in range(DIFF_HEADS):
        sl = slice(_T_DK + hd * LANES, _T_DK + (hd + 1) * LANES)
        dk_ref[:, hd * LANES:(hd + 1) * LANES] = _rope_lanes(tok[:, sl], cb, sbu, sbd, rh).astype(BF16)
    gb_ref[...] = _silu(tok[:, _T_GB:_T_GB + 512]).astype(BF16)


def _softmax_pv(s_t, vt, m_ref, acc_ref, idx, lanes):
    m_old = m_ref[idx, :, lanes]
    m_new = jnp.maximum(m_old, jnp.max(s_t, axis=0, keepdims=True))
    alpha = jnp.exp2(m_old - m_new)
    p_t = jnp.exp2(s_t - m_new).astype(BF16)
    pv = jnp.dot(vt, p_t, preferred_element_type=F32)
    acc_ref[idx, :, lanes] = alpha * acc_ref[idx, :, lanes] + pv
    m_ref[idx, :, lanes] = m_new


def _run_pipelined(units, ready, following):
    pending = list(ready)
    stream = [scores for scores, _ in units] + [scores for scores, _ in following]
    for n, (_, update) in enumerate(units):
        if n + SCORE_LOOKAHEAD < len(stream):
            pending.append(stream[n + SCORE_LOOKAHEAD]())
        update(pending.pop(0))
    return pending


def _attn_kernel(qat_ref, dq1t_ref, dq2t_ref, ga_ref, gb_ref, x_ref,
                 ka_ref, vat_ref, dk_ref, dvt_ref, wout_ref,
                 lq1_ref, lk1_ref, lq2_ref, lk2_ref, subg_ref, postg_ref,
                 out_ref, m_ref, acc_ref, carry_ref, *, lambda_init):
    qi = pl.program_id(1)
    m_ref[...] = jnp.full(m_ref.shape, -jnp.inf, F32)
    acc_ref[...] = jnp.zeros(acc_ref.shape, F32)

    ones = jnp.ones((BF16_ROWS, KV_TILE), BF16)
    key = lax.broadcasted_iota(jnp.int32, (KV_TILE, Q_TILE), 0)
    qry = lax.broadcasted_iota(jnp.int32, (KV_TILE, Q_TILE), 1)
    causal = key <= qry
    all_q = slice(0, Q_TILE)
    upper_q = slice(KV_TILE, Q_TILE)

    def tile_units(j, lanes, mask):
        rows = pl.ds(pl.multiple_of(j * KV_TILE, KV_TILE), KV_TILE)
        units = []
        for u in range(N_MAPS):
            if u < MLA_HEADS:
                cs = slice(u * MLA_QK_PAD, (u + 1) * MLA_QK_PAD)
                vs = slice(u * MLA_V, (u + 1) * MLA_V)
                k_ref, qt_ref, v_ref = ka_ref, qat_ref, vat_ref
            else:
                d = u - MLA_HEADS
                cs = vs = slice((d // 2) * LANES, (d // 2 + 1) * LANES)
                k_ref, qt_ref, v_ref = dk_ref, (dq2t_ref if d % 2 else dq1t_ref), dvt_ref

            def scores(cs=cs, k_ref=k_ref, qt_ref=qt_ref):
                return jnp.dot(k_ref[rows, cs], qt_ref[cs, lanes], preferred_element_type=F32)

            def update(s_t, u=u, vs=vs, v_ref=v_ref):
                if mask is not None:
                    s_t = jnp.where(mask, s_t, -jnp.inf)
                vt = jnp.concatenate([v_ref[j, vs, :], ones], axis=0)
                _softmax_pv(s_t, vt, m_ref, acc_ref, u, lanes)

            units.append((scores, update))
        return units

    def load_carry():
        return [carry_ref[n] for n in range(SCORE_LOOKAHEAD)]

    def store_carry(scores):
        for n, s_t in enumerate(scores):
            carry_ref[n] = s_t

    store_carry([scores() for scores, _ in tile_units(0, all_q, None)[:SCORE_LOOKAHEAD]])

    def body(i, carry):
        units = tile_units(2 * i, all_q, None) + tile_units(2 * i + 1, all_q, None)
        store_carry(_run_pipelined(units, load_carry(), tile_units(2 * i + 2, all_q, None)))
        return carry

    lax.fori_loop(0, qi, body, 0)
    _run_pipelined(tile_units(2 * qi, all_q, causal)
                   + tile_units(2 * qi + 1, upper_q, causal[:, :KV_TILE]), load_carry(), [])

    lam = (jnp.exp(jnp.sum(lq1_ref[...] * lk1_ref[...], axis=1, keepdims=True))
           - jnp.exp(jnp.sum(lq2_ref[...] * lk2_ref[...], axis=1, keepdims=True))
           + lambda_init)

    def normalised(u):
        acc = acc_ref[u]
        return acc[:MLA_V] * (1.0 / acc[MLA_V:MLA_V + 1])

    oa = jnp.concatenate([normalised(h).T for h in range(MLA_HEADS)], axis=1)
    oa = oa * ga_ref[...].astype(F32)
    ob = []
    for h in range(DIFF_HEADS):
        o_t = normalised(MLA_HEADS + 2 * h) - lam * normalised(MLA_HEADS + 2 * h + 1)
        o_t = o_t * lax.rsqrt(jnp.mean(o_t * o_t, axis=0, keepdims=True) + SUBLN_EPS)
        ob.append(o_t.T * subg_ref[...] * (1.0 - lambda_init))
    ob = jnp.concatenate(ob, axis=1) * gb_ref[...].astype(F32)

    mixed_in = jnp.concatenate([oa, ob], axis=1).astype(BF16)
    mixed = jnp.dot(mixed_in, wout_ref[...], preferred_element_type=F32)
    y = mixed * lax.rsqrt(jnp.mean(mixed * mixed, axis=-1, keepdims=True) + NORM_EPS)
    out_ref[...] = x_ref[...] + y * postg_ref[...]


def _layer(x3, ln_pre_g, w_in, kv_norm_g, w_uk, w_uv, lq1, lk1, lq2, lk2,
           subln_g, w_out, ln_post_g, tables, lambda_init):
    batch, seq, d_model = x3.shape
    w_tok, w_feat = _split_weights(w_in)
    g2 = ln_pre_g.reshape(1, d_model)
    kvg2 = kv_norm_g.reshape(1, KV_RANK)
    wuk = w_uk.astype(BF16)
    wuvt = w_uv.T.astype(BF16)

    seq_tiles = seq // PROJ_TILE
    kv_per_tile = PROJ_TILE // KV_TILE
    n_kv = seq // KV_TILE
    tok_blk = lambda w: pl.BlockSpec((None, PROJ_TILE, w), lambda b, t: (b, t, 0))
    feat_blk = lambda r: pl.BlockSpec((None, r, PROJ_TILE), lambda b, t: (b, 0, t))
    vt_blk = pl.BlockSpec((None, kv_per_tile, 512, KV_TILE), lambda b, t: (b, t, 0, 0))
    full = lambda a: pl.BlockSpec(a.shape, lambda b, t: (0,) * a.ndim)
    tab_tok = pl.BlockSpec((PROJ_TILE, LANES), lambda b, t: (t, 0))
    tab_feat = lambda a: pl.BlockSpec((a.shape[0], PROJ_TILE), lambda b, t: (0, t))
    tok_tabs, feat_tabs = tables[:6], tables[6:]
    sds = jax.ShapeDtypeStruct
    qat, dq1t, dq2t, vat, dvt, ka, dk, ga, gb = pl.pallas_call(
        _proj_kernel,
        grid=(batch, seq_tiles),
        in_specs=[tok_blk(d_model), full(g2), full(w_tok), full(w_feat), full(kvg2),
                  full(wuk), full(wuvt)] + [tab_tok] * 6 + [tab_feat(a) for a in feat_tabs],
        out_specs=[feat_blk(1024), feat_blk(512), feat_blk(512), vt_blk, vt_blk,
                   tok_blk(1024), tok_blk(512), tok_blk(512), tok_blk(512)],
        out_shape=[sds((batch, 1024, seq), BF16), sds((batch, 512, seq), BF16),
                   sds((batch, 512, seq), BF16),
                   sds((batch, n_kv, 512, KV_TILE), BF16), sds((batch, n_kv, 512, KV_TILE), BF16),
                   sds((batch, seq, 1024), BF16), sds((batch, seq, 512), BF16),
                   sds((batch, seq, 512), BF16), sds((batch, seq, 512), BF16)],
        compiler_params=pltpu.CompilerParams(
            dimension_semantics=("arbitrary", "arbitrary"), vmem_limit_bytes=VMEM_LIMIT_BYTES),
        name="proj",
    )(x3, g2, w_tok, w_feat, kvg2, wuk, wuvt, *tok_tabs, *feat_tabs)

    q_tok = lambda w: pl.BlockSpec((None, Q_TILE, w), lambda b, i: (b, i, 0))
    q_feat = lambda r: pl.BlockSpec((None, r, Q_TILE), lambda b, i: (b, 0, i))
    kv_tok = lambda w: pl.BlockSpec((None, seq, w), lambda b, i: (b, 0, 0))
    kv_feat = pl.BlockSpec((None, n_kv, 512, KV_TILE), lambda b, i: (b, 0, 0, 0))
    fullb = lambda a: pl.BlockSpec(a.shape, lambda b, i: (0,) * a.ndim)
    wout = w_out.astype(BF16)
    lvecs = [v.reshape(1, DIFF_QK) for v in (lq1, lk1, lq2, lk2)]
    subg2 = subln_g.reshape(1, DIFF_V)
    postg2 = ln_post_g.reshape(1, d_model)
    return pl.pallas_call(
        functools.partial(_attn_kernel, lambda_init=lambda_init),
        grid=(batch, seq // Q_TILE),
        in_specs=[q_feat(1024), q_feat(512), q_feat(512), q_tok(512), q_tok(512), q_tok(d_model),
                  kv_tok(1024), kv_feat, kv_tok(512), kv_feat, fullb(wout)]
                 + [fullb(v) for v in lvecs] + [fullb(subg2), fullb(postg2)],
        out_specs=q_tok(d_model),
        out_shape=sds((batch, seq, d_model), F32),
        scratch_shapes=[
            pltpu.VMEM((N_MAPS, 1, Q_TILE), F32),
            pltpu.VMEM((N_MAPS, ACC_ROWS, Q_TILE), F32),
            pltpu.VMEM((SCORE_LOOKAHEAD, KV_TILE, Q_TILE), F32),
        ],
        compiler_params=pltpu.CompilerParams(
            dimension_semantics=("arbitrary", "arbitrary"), vmem_limit_bytes=VMEM_LIMIT_BYTES),
        name="attn",
    )(qat, dq1t, dq2t, ga, gb, x3, ka, vat, dk, dvt, wout, *lvecs, subg2, postg2)


def kernel(x, ln_pre_g, w_in, kv_norm_g, w_uk, w_uv, lambda_q1, lambda_k1,
           lambda_q2, lambda_k2, subln_g, w_out, ln_post_g):
    batch, seq, d_model = x.shape
    depth = w_in.shape[0]
    assert seq % PROJ_TILE == 0 and PROJ_TILE % KV_TILE == 0 and seq % Q_TILE == 0
    assert w_in.shape[2] == _C_END and d_model == 1024
    tables = _rope_tables(seq)
    for l in range(depth):
        lambda_init = 0.8 - 0.6 * math.exp(-0.3 * l)
        x = _layer(x, ln_pre_g[l], w_in[l], kv_norm_g[l], w_uk[l], w_uv[l],
                   lambda_q1[l], lambda_k1[l], lambda_q2[l], lambda_k2[l], subln_g[l],
                   w_out[l], ln_post_g[l], tables, lambda_init)
    return x
```

```python
import functools
import math

import jax
import jax.numpy as jnp
import numpy as np
from jax import lax
from jax.experimental import pallas as pl
from jax.experimental.pallas import tpu as pltpu

ROPE_THETA = 500000.0
NORM_EPS = 1e-6
SUBLN_EPS = 1e-5

MLA_HEADS = 4
MLA_NOPE = 128
MLA_ROPE = 64
MLA_V = 128
KV_RANK = 128
DIFF_HEADS = 4
DIFF_QK = 64
DIFF_V = 2 * DIFF_QK
DIFF_ROPE = DIFF_QK // 4
MIX_WIDTH = MLA_HEADS * MLA_V + DIFF_HEADS * DIFF_V

LANES = 128
BF16_ROWS = 16
MLA_QK_PAD = 2 * LANES
LOG2E = math.log2(math.e)
MLA_QSCALE = LOG2E / math.sqrt(MLA_NOPE + MLA_ROPE)
DIFF_QSCALE = LOG2E / math.sqrt(DIFF_QK)

PROJ_TILE = 512
KV_TILE = 256
Q_TILE = 2 * KV_TILE
N_MAPS = MLA_HEADS + 2 * DIFF_HEADS
ACC_ROWS = MLA_V + BF16_ROWS
SCORE_LOOKAHEAD = 2
OUT_TILE = 1024
OUT_CHUNK = 512
VMEM_LIMIT_BYTES = 56 * 1024 * 1024

BF16 = jnp.bfloat16
F32 = jnp.float32
_NT = (((1,), (1,)), ((), ()))

_C_QN = 0
_C_QR = _C_QN + MLA_HEADS * MLA_NOPE
_C_CKV = _C_QR + MLA_HEADS * MLA_ROPE
_C_KR = _C_CKV + KV_RANK
_C_GA = _C_KR + MLA_ROPE
_C_DQ = _C_GA + MLA_HEADS * MLA_V
_C_DK = _C_DQ + DIFF_HEADS * 2 * DIFF_QK
_C_DV = _C_DK + DIFF_HEADS * 2 * DIFF_QK
_C_GB = _C_DV + DIFF_HEADS * DIFF_V
_C_END = _C_GB + DIFF_HEADS * DIFF_V
_T_CKV, _T_KR, _T_DK = 0, 128, 256
_F_QN, _F_QR, _F_DQ, _F_DV, _F_GA, _F_GB, _F_END = 0, 512, 768, 1280, 1792, 2304, 2816


def _split_weights(w_in):
    pad = jnp.zeros((w_in.shape[0], LANES - MLA_ROPE), w_in.dtype)
    w_tok = jnp.concatenate([w_in[:, _C_CKV:_C_GA], pad, w_in[:, _C_DK:_C_DV]], axis=1)
    w_feat = jnp.concatenate([w_in[:, _C_QN:_C_CKV], w_in[:, _C_DQ:_C_DK], w_in[:, _C_DV:_C_GB],
                              w_in[:, _C_GA:_C_DQ], w_in[:, _C_GB:_C_END]], axis=1).T
    return w_tok.astype(BF16), w_feat.astype(BF16)


def _rope_tables(seq):
    def tab(dim):
        inv_freq = ROPE_THETA ** (-np.arange(0, dim, 2, dtype=np.float64) / dim)
        ang = np.arange(seq, dtype=np.float64)[:, None] * inv_freq[None, :]
        return np.cos(ang), np.sin(ang)

    def lane_tables(c, s, group, rest_cos):
        half = c.shape[1]
        rest = np.full((seq, group - 2 * half), rest_cos)
        zero = np.zeros((seq, group - 2 * half))
        zh = np.zeros((seq, half))
        reps = LANES // group
        return (np.concatenate([c, c, rest] * reps, axis=1),
                np.concatenate([-s, zh, zero] * reps, axis=1),
                np.concatenate([zh, s, zero] * reps, axis=1))

    ca, sa = tab(MLA_ROPE)
    cb, sb = tab(DIFF_ROPE)
    tables = lane_tables(ca, sa, LANES, 0.0)
    tables += lane_tables(cb, sb, DIFF_QK, 1.0)
    tables += (ca.T, sa.T, cb.T, sb.T)
    return tuple(jnp.asarray(t, F32) for t in tables)


def _silu(g):
    return g / (1.0 + jnp.exp(-g))


def _rope_lanes(v, cos, sin_up, sin_down, half):
    return (v * cos + pltpu.roll(v, LANES - half, axis=1) * sin_up
            + pltpu.roll(v, half, axis=1) * sin_down)


def _proj_kernel(x_ref, g_ref, wtok_ref, wfeat_ref, kvg_ref, wuk_ref, wuvt_ref,
                 ca_ref, sau_ref, sad_ref, cb_ref, sbu_ref, sbd_ref,
                 cat_ref, sat_ref, cbt_ref, sbt_ref,
                 qat_ref, dq1t_ref, dq2t_ref, vat_ref, dvt_ref, gat_ref, gbt_ref, ka_ref, dk_ref):
    x = x_ref[...]
    ms = jnp.mean(x * x, axis=-1, keepdims=True)
    h = (x * lax.rsqrt(ms + NORM_EPS) * g_ref[...]).astype(BF16)

    tok = jnp.dot(h, wtok_ref[...], preferred_element_type=F32)
    feat = lax.dot_general(wfeat_ref[...], h, _NT, preferred_element_type=F32)

    cat, sat = cat_ref[...], sat_ref[...]
    half = MLA_ROPE // 2
    zpad = jnp.zeros((MLA_QK_PAD - MLA_NOPE - MLA_ROPE, PROJ_TILE), BF16)
    for hd in range(MLA_HEADS):
        base = hd * MLA_QK_PAD
        qn = feat[_F_QN + hd * MLA_NOPE:_F_QN + (hd + 1) * MLA_NOPE]
        qat_ref[base:base + MLA_NOPE, :] = (qn * MLA_QSCALE).astype(BF16)
        x1 = feat[_F_QR + hd * MLA_ROPE:_F_QR + hd * MLA_ROPE + half]
        x2 = feat[_F_QR + hd * MLA_ROPE + half:_F_QR + (hd + 1) * MLA_ROPE]
        r1 = (x1 * cat - x2 * sat) * MLA_QSCALE
        r2 = (x2 * cat + x1 * sat) * MLA_QSCALE
        rope_rows = jnp.concatenate([r1.astype(BF16), r2.astype(BF16), zpad], axis=0)
        qat_ref[base + MLA_NOPE:base + MLA_QK_PAD, :] = rope_rows

    cbt, sbt = cbt_ref[...], sbt_ref[...]
    rh = DIFF_ROPE // 2
    zmap = jnp.zeros((DIFF_QK, PROJ_TILE), BF16)
    for hd in range(DIFF_HEADS):
        maps = []
        for mp in range(2):
            d = feat[_F_DQ + (2 * hd + mp) * DIFF_QK:_F_DQ + (2 * hd + mp + 1) * DIFF_QK]
            x1, x2 = d[:rh], d[rh:2 * rh]
            q = jnp.concatenate([x1 * cbt - x2 * sbt, x2 * cbt + x1 * sbt, d[2 * rh:]], axis=0)
            maps.append((q * DIFF_QSCALE).astype(BF16))
        sl = slice(hd * LANES, (hd + 1) * LANES)
        dq1t_ref[sl, :] = jnp.concatenate([maps[0], zmap], axis=0)
        dq2t_ref[sl, :] = jnp.concatenate([zmap, maps[1]], axis=0)
    for c in range(PROJ_TILE // KV_TILE):
        dvt_ref[c] = feat[_F_DV:_F_GA, c * KV_TILE:(c + 1) * KV_TILE].astype(BF16)
    gat_ref[...] = _silu(feat[_F_GA:_F_GB]).astype(BF16)
    gbt_ref[...] = _silu(feat[_F_GB:_F_END]).astype(BF16)

    c = tok[:, _T_CKV:_T_CKV + KV_RANK]
    c = c * lax.rsqrt(jnp.mean(c * c, axis=-1, keepdims=True) + NORM_EPS) * kvg_ref[...]
    c = c.astype(BF16)
    kn = jnp.dot(c, wuk_ref[...], preferred_element_type=F32)
    vt = lax.dot_general(wuvt_ref[...], c, _NT, preferred_element_type=F32)
    for ci in range(PROJ_TILE // KV_TILE):
        vat_ref[ci] = vt[:, ci * KV_TILE:(ci + 1) * KV_TILE].astype(BF16)
    kr = _rope_lanes(tok[:, _T_KR:_T_KR + LANES], ca_ref[...], sau_ref[...], sad_ref[...], half)
    kr = kr.astype(BF16)
    for hd in range(MLA_HEADS):
        base = hd * MLA_QK_PAD
        ka_ref[:, base:base + LANES] = kn[:, hd * LANES:(hd + 1) * LANES].astype(BF16)
        ka_ref[:, base + LANES:base + 2 * LANES] = kr
    cb, sbu, sbd = cb_ref[...], sbu_ref[...], sbd_ref[...]
    for hd in range(DIFF_HEADS):
        sl = slice(_T_DK + hd * LANES, _T_DK + (hd + 1) * LANES)
        dk_ref[:, hd * LANES:(hd + 1) * LANES] = _rope_lanes(tok[:, sl], cb, sbu, sbd, rh).astype(BF16)


def _softmax_pv(s_t, vt, m_ref, acc_ref, idx, lanes):
    m_old = m_ref[idx, :, lanes]
    m_new = jnp.maximum(m_old, jnp.max(s_t, axis=0, keepdims=True))
    alpha = jnp.exp2(m_old - m_new)
    p_t = jnp.exp2(s_t - m_new).astype(BF16)
    pv = jnp.dot(vt, p_t, preferred_element_type=F32)
    acc_ref[idx, :, lanes] = alpha * acc_ref[idx, :, lanes] + pv
    m_ref[idx, :, lanes] = m_new


def _run_pipelined(units, ready, following):
    pending = list(ready)
    stream = [scores for scores, _ in units] + [scores for scores, _ in following]
    for n, (_, update) in enumerate(units):
        if n + SCORE_LOOKAHEAD < len(stream):
            pending.append(stream[n + SCORE_LOOKAHEAD]())
        update(pending.pop(0))
    return pending


def _attn_kernel(qat_ref, dq1t_ref, dq2t_ref, gat_ref, gbt_ref,
                 ka_ref, vat_ref, dk_ref, dvt_ref,
                 lq1_ref, lk1_ref, lq2_ref, lk2_ref, subg_ref,
                 ot_ref, m_ref, acc_ref, carry_ref, *, lambda_init):
    qi = pl.program_id(1)
    m_ref[...] = jnp.full(m_ref.shape, -jnp.inf, F32)
    acc_ref[...] = jnp.zeros(acc_ref.shape, F32)

    ones = jnp.ones((BF16_ROWS, KV_TILE), BF16)
    key = lax.broadcasted_iota(jnp.int32, (KV_TILE, Q_TILE), 0)
    qry = lax.broadcasted_iota(jnp.int32, (KV_TILE, Q_TILE), 1)
    causal = key <= qry
    all_q = slice(0, Q_TILE)
    upper_q = slice(KV_TILE, Q_TILE)

    def tile_units(j, lanes, mask):
        rows = pl.ds(pl.multiple_of(j * KV_TILE, KV_TILE), KV_TILE)
        units = []
        for u in range(N_MAPS):
            if u < MLA_HEADS:
                cs = slice(u * MLA_QK_PAD, (u + 1) * MLA_QK_PAD)
                vs = slice(u * MLA_V, (u + 1) * MLA_V)
                k_ref, qt_ref, v_ref = ka_ref, qat_ref, vat_ref
            else:
                d = u - MLA_HEADS
                cs = vs = slice((d // 2) * LANES, (d // 2 + 1) * LANES)
                k_ref, qt_ref, v_ref = dk_ref, (dq2t_ref if d % 2 else dq1t_ref), dvt_ref

            def scores(cs=cs, k_ref=k_ref, qt_ref=qt_ref):
                return jnp.dot(k_ref[rows, cs], qt_ref[cs, lanes], preferred_element_type=F32)

            def update(s_t, u=u, vs=vs, v_ref=v_ref):
                if mask is not None:
                    s_t = jnp.where(mask, s_t, -jnp.inf)
                vt = jnp.concatenate([v_ref[j, vs, :], ones], axis=0)
                _softmax_pv(s_t, vt, m_ref, acc_ref, u, lanes)

            units.append((scores, update))
        return units

    def load_carry():
        return [carry_ref[n] for n in range(SCORE_LOOKAHEAD)]

    def store_carry(scores):
        for n, s_t in enumerate(scores):
            carry_ref[n] = s_t

    store_carry([scores() for scores, _ in tile_units(0, all_q, None)[:SCORE_LOOKAHEAD]])

    def body(i, carry):
        units = tile_units(2 * i, all_q, None) + tile_units(2 * i + 1, all_q, None)
        store_carry(_run_pipelined(units, load_carry(), tile_units(2 * i + 2, all_q, None)))
        return carry

    lax.fori_loop(0, qi, body, 0)
    _run_pipelined(tile_units(2 * qi, all_q, causal)
                   + tile_units(2 * qi + 1, upper_q, causal[:, :KV_TILE]), load_carry(), [])

    lam = (jnp.exp(jnp.sum(lq1_ref[...] * lk1_ref[...], axis=1, keepdims=True))
           - jnp.exp(jnp.sum(lq2_ref[...] * lk2_ref[...], axis=1, keepdims=True))
           + lambda_init)

    def normalised(u):
        acc = acc_ref[u]
        return acc[:MLA_V] * (1.0 / acc[MLA_V:MLA_V + 1])

    for h in range(MLA_HEADS):
        rows = slice(h * MLA_V, (h + 1) * MLA_V)
        ot_ref[rows, :] = (normalised(h) * gat_ref[rows, :].astype(F32)).astype(BF16)
    sub_gain = subg_ref[...] * (1.0 - lambda_init)
    for h in range(DIFF_HEADS):
        rows = slice(h * DIFF_V, (h + 1) * DIFF_V)
        o_t = normalised(MLA_HEADS + 2 * h) - lam * normalised(MLA_HEADS + 2 * h + 1)
        o_t = o_t * lax.rsqrt(jnp.mean(o_t * o_t, axis=0, keepdims=True) + SUBLN_EPS)
        o_t = o_t * sub_gain * gbt_ref[rows, :].astype(F32)
        ot_ref[MLA_HEADS * MLA_V + h * DIFF_V:MLA_HEADS * MLA_V + (h + 1) * DIFF_V, :] = (
            o_t.astype(BF16))


def _out_kernel(ot_ref, x_ref, wout_ref, postg_ref, out_ref):
    for c in range(OUT_TILE // OUT_CHUNK):
        rows = slice(c * OUT_CHUNK, (c + 1) * OUT_CHUNK)
        mixed_in = ot_ref[:, rows].astype(F32).T.astype(BF16)
        mixed = jnp.dot(mixed_in, wout_ref[...], preferred_element_type=F32)
        y = mixed * lax.rsqrt(jnp.mean(mixed * mixed, axis=-1, keepdims=True) + NORM_EPS)
        out_ref[rows, :] = x_ref[rows, :] + y * postg_ref[...]


def _layer(x3, ln_pre_g, w_in, kv_norm_g, w_uk, w_uv, lq1, lk1, lq2, lk2,
           subln_g, w_out, ln_post_g, tables, lambda_init):
    batch, seq, d_model = x3.shape
    w_tok, w_feat = _split_weights(w_in)
    g2 = ln_pre_g.reshape(1, d_model)
    kvg2 = kv_norm_g.reshape(1, KV_RANK)
    wuk = w_uk.astype(BF16)
    wuvt = w_uv.T.astype(BF16)

    seq_tiles = seq // PROJ_TILE
    kv_per_tile = PROJ_TILE // KV_TILE
    n_kv = seq // KV_TILE
    tok_blk = lambda w: pl.BlockSpec((None, PROJ_TILE, w), lambda b, t: (b, t, 0))
    feat_blk = lambda r: pl.BlockSpec((None, r, PROJ_TILE), lambda b, t: (b, 0, t))
    vt_blk = pl.BlockSpec((None, kv_per_tile, 512, KV_TILE), lambda b, t: (b, t, 0, 0))
    full = lambda a: pl.BlockSpec(a.shape, lambda b, t: (0,) * a.ndim)
    tab_tok = pl.BlockSpec((PROJ_TILE, LANES), lambda b, t: (t, 0))
    tab_feat = lambda a: pl.BlockSpec((a.shape[0], PROJ_TILE), lambda b, t: (0, t))
    tok_tabs, feat_tabs = tables[:6], tables[6:]
    sds = jax.ShapeDtypeStruct
    feat_out = lambda r: sds((batch, r, seq), BF16)
    qat, dq1t, dq2t, vat, dvt, gat, gbt, ka, dk = pl.pallas_call(
        _proj_kernel,
        grid=(batch, seq_tiles),
        in_specs=[tok_blk(d_model), full(g2), full(w_tok), full(w_feat), full(kvg2),
                  full(wuk), full(wuvt)] + [tab_tok] * 6 + [tab_feat(a) for a in feat_tabs],
        out_specs=[feat_blk(1024), feat_blk(512), feat_blk(512), vt_blk, vt_blk,
                   feat_blk(512), feat_blk(512), tok_blk(1024), tok_blk(512)],
        out_shape=[feat_out(1024), feat_out(512), feat_out(512),
                   sds((batch, n_kv, 512, KV_TILE), BF16), sds((batch, n_kv, 512, KV_TILE), BF16),
                   feat_out(512), feat_out(512),
                   sds((batch, seq, 1024), BF16), sds((batch, seq, 512), BF16)],
        compiler_params=pltpu.CompilerParams(
            dimension_semantics=("arbitrary", "arbitrary"), vmem_limit_bytes=VMEM_LIMIT_BYTES),
        name="proj",
    )(x3, g2, w_tok, w_feat, kvg2, wuk, wuvt, *tok_tabs, *feat_tabs)

    q_feat = lambda r: pl.BlockSpec((None, r, Q_TILE), lambda b, i: (b, 0, i))
    kv_tok = lambda w: pl.BlockSpec((None, seq, w), lambda b, i: (b, 0, 0))
    kv_feat = pl.BlockSpec((None, n_kv, 512, KV_TILE), lambda b, i: (b, 0, 0, 0))
    fullb = lambda a: pl.BlockSpec(a.shape, lambda b, i: (0,) * a.ndim)
    lvecs = [v.reshape(1, DIFF_QK) for v in (lq1, lk1, lq2, lk2)]
    subg_col = subln_g.reshape(DIFF_V, 1)
    o_t = pl.pallas_call(
        functools.partial(_attn_kernel, lambda_init=lambda_init),
        grid=(batch, seq // Q_TILE),
        in_specs=[q_feat(1024), q_feat(512), q_feat(512), q_feat(512), q_feat(512),
                  kv_tok(1024), kv_feat, kv_tok(512), kv_feat]
                 + [fullb(v) for v in lvecs] + [fullb(subg_col)],
        out_specs=q_feat(MIX_WIDTH),
        out_shape=feat_out(MIX_WIDTH),
        scratch_shapes=[
            pltpu.VMEM((N_MAPS, 1, Q_TILE), F32),
            pltpu.VMEM((N_MAPS, ACC_ROWS, Q_TILE), F32),
            pltpu.VMEM((SCORE_LOOKAHEAD, KV_TILE, Q_TILE), F32),
        ],
        compiler_params=pltpu.CompilerParams(
            dimension_semantics=("arbitrary", "arbitrary"), vmem_limit_bytes=VMEM_LIMIT_BYTES),
        name="attn",
    )(qat, dq1t, dq2t, gat, gbt, ka, vat, dk, dvt, *lvecs, subg_col)

    wout = w_out.astype(BF16)
    postg2 = ln_post_g.reshape(1, d_model)
    fullo = lambda a: pl.BlockSpec(a.shape, lambda b, t: (0,) * a.ndim)
    return pl.pallas_call(
        _out_kernel,
        grid=(batch, seq // OUT_TILE),
        in_specs=[pl.BlockSpec((None, MIX_WIDTH, OUT_TILE), lambda b, t: (b, 0, t)),
                  pl.BlockSpec((None, OUT_TILE, d_model), lambda b, t: (b, t, 0)),
                  fullo(wout), fullo(postg2)],
        out_specs=pl.BlockSpec((None, OUT_TILE, d_model), lambda b, t: (b, t, 0)),
        out_shape=sds((batch, seq, d_model), F32),
        compiler_params=pltpu.CompilerParams(
            dimension_semantics=("arbitrary", "arbitrary"), vmem_limit_bytes=VMEM_LIMIT_BYTES),
        name="outproj",
    )(o_t, x3, wout, postg2)


def kernel(x, ln_pre_g, w_in, kv_norm_g, w_uk, w_uv, lambda_q1, lambda_k1,
           lambda_q2, lambda_k2, subln_g, w_out, ln_post_g):
    batch, seq, d_model = x.shape
    depth = w_in.shape[0]
    assert seq % PROJ_TILE == 0 and PROJ_TILE % KV_TILE == 0 and seq % Q_TILE == 0
    assert seq % OUT_TILE == 0 and OUT_TILE % OUT_CHUNK == 0
    assert w_in.shape[2] == _C_END and d_model == 1024
    tables = _rope_tables(seq)
    for l in range(depth):
        lambda_init = 0.8 - 0.6 * math.exp(-0.3 * l)
        x = _layer(x, ln_pre_g[l], w_in[l], kv_norm_g[l], w_uk[l], w_uv[l],
                   lambda_q1[l], lambda_k1[l], lambda_q2[l], lambda_k2[l], subln_g[l],
                   w_out[l], ln_post_g[l], tables, lambda_init)
    return x
```

```python
import functools
import math

import jax
import jax.numpy as jnp
import numpy as np
from jax import lax
from jax.experimental import pallas as pl
from jax.experimental.pallas import tpu as pltpu

ROPE_THETA = 500000.0
NORM_EPS = 1e-6
SUBLN_EPS = 1e-5

MLA_HEADS = 4
MLA_NOPE = 128
MLA_ROPE = 64
MLA_V = 128
KV_RANK = 128
DIFF_HEADS = 4
DIFF_QK = 64
DIFF_V = 2 * DIFF_QK
DIFF_ROPE = DIFF_QK // 4

LANES = 128
BF16_ROWS = 16
MLA_QK_PAD = 2 * LANES
LOG2E = math.log2(math.e)
MLA_QSCALE = LOG2E / math.sqrt(MLA_NOPE + MLA_ROPE)
DIFF_QSCALE = LOG2E / math.sqrt(DIFF_QK)

PROJ_TILE = 512
KV_TILE = 256
Q_TILE = 2 * KV_TILE
N_MAPS = MLA_HEADS + 2 * DIFF_HEADS
ACC_ROWS = MLA_V + BF16_ROWS
SCORE_LOOKAHEAD = 2
VMEM_LIMIT_BYTES = 56 * 1024 * 1024

BF16 = jnp.bfloat16
F32 = jnp.float32
_NT = (((1,), (1,)), ((), ()))

_C_QN = 0
_C_QR = _C_QN + MLA_HEADS * MLA_NOPE
_C_CKV = _C_QR + MLA_HEADS * MLA_ROPE
_C_KR = _C_CKV + KV_RANK
_C_GA = _C_KR + MLA_ROPE
_C_DQ = _C_GA + MLA_HEADS * MLA_V
_C_DK = _C_DQ + DIFF_HEADS * 2 * DIFF_QK
_C_DV = _C_DK + DIFF_HEADS * 2 * DIFF_QK
_C_GB = _C_DV + DIFF_HEADS * DIFF_V
_C_END = _C_GB + DIFF_HEADS * DIFF_V
_T_CKV, _T_KR, _T_GA, _T_DK, _T_GB = 0, 128, 256, 768, 1280
_F_QN, _F_QR, _F_DQ, _F_DV = 0, 512, 768, 1280


def _split_weights(w_in):
    pad = jnp.zeros((w_in.shape[0], LANES - MLA_ROPE), w_in.dtype)
    w_tok = jnp.concatenate([w_in[:, _C_CKV:_C_GA], pad, w_in[:, _C_GA:_C_DQ],
                             w_in[:, _C_DK:_C_DV], w_in[:, _C_GB:_C_END]], axis=1)
    w_feat = jnp.concatenate([w_in[:, _C_QN:_C_CKV], w_in[:, _C_DQ:_C_DK],
                              w_in[:, _C_DV:_C_GB]], axis=1)
    return w_tok.astype(BF16), w_feat.astype(BF16)


def _rope_tables(seq):
    def tab(dim):
        inv_freq = ROPE_THETA ** (-np.arange(0, dim, 2, dtype=np.float64) / dim)
        ang = np.arange(seq, dtype=np.float64)[:, None] * inv_freq[None, :]
        return np.cos(ang), np.sin(ang)

    def lane_tables(c, s, group, rest_cos):
        half = c.shape[1]
        rest = np.full((seq, group - 2 * half), rest_cos)
        zero = np.zeros((seq, group - 2 * half))
        zh = np.zeros((seq, half))
        reps = LANES // group
        return (np.concatenate([c, c, rest] * reps, axis=1),
                np.concatenate([-s, zh, zero] * reps, axis=1),
                np.concatenate([zh, s, zero] * reps, axis=1))

    ca, sa = tab(MLA_ROPE)
    cb, sb = tab(DIFF_ROPE)
    tables = lane_tables(ca, sa, LANES, 0.0)
    tables += lane_tables(cb, sb, DIFF_QK, 1.0)
    tables += (ca.T, sa.T, cb.T, sb.T)
    return tuple(jnp.asarray(t, F32) for t in tables)


def _silu(g):
    return g / (1.0 + jnp.exp(-g))


def _rope_lanes(v, cos, sin_up, sin_down, half):
    return (v * cos + pltpu.roll(v, LANES - half, axis=1) * sin_up
            + pltpu.roll(v, half, axis=1) * sin_down)


def _proj_kernel(x_ref, g_ref, wtok_ref, wfeat_ref, kvg_ref, wuk_ref, wuvt_ref,
                 ca_ref, sau_ref, sad_ref, cb_ref, sbu_ref, sbd_ref,
                 cat_ref, sat_ref, cbt_ref, sbt_ref,
                 qat_ref, dq1t_ref, dq2t_ref, vat_ref, dvt_ref, ka_ref, dk_ref, ga_ref, gb_ref):
    x = x_ref[...]
    ms = jnp.mean(x * x, axis=-1, keepdims=True)
    h = (x * lax.rsqrt(ms + NORM_EPS) * g_ref[...]).astype(BF16)

    tok = jnp.dot(h, wtok_ref[...], preferred_element_type=F32)
    feat = jnp.dot(h, wfeat_ref[...], preferred_element_type=F32).T

    cat, sat = cat_ref[...], sat_ref[...]
    half = MLA_ROPE // 2
    zpad = jnp.zeros((MLA_QK_PAD - MLA_NOPE - MLA_ROPE, PROJ_TILE), BF16)
    for hd in range(MLA_HEADS):
        base = hd * MLA_QK_PAD
        qn = feat[_F_QN + hd * MLA_NOPE:_F_QN + (hd + 1) * MLA_NOPE]
        qat_ref[base:base + MLA_NOPE, :] = (qn * MLA_QSCALE).astype(BF16)
        x1 = feat[_F_QR + hd * MLA_ROPE:_F_QR + hd * MLA_ROPE + half]
        x2 = feat[_F_QR + hd * MLA_ROPE + half:_F_QR + (hd + 1) * MLA_ROPE]
        r1 = (x1 * cat - x2 * sat) * MLA_QSCALE
        r2 = (x2 * cat + x1 * sat) * MLA_QSCALE
        rope_rows = jnp.concatenate([r1.astype(BF16), r2.astype(BF16), zpad], axis=0)
        qat_ref[base + MLA_NOPE:base + MLA_QK_PAD, :] = rope_rows

    cbt, sbt = cbt_ref[...], sbt_ref[...]
    rh = DIFF_ROPE // 2
    zmap = jnp.zeros((DIFF_QK, PROJ_TILE), BF16)
    for hd in range(DIFF_HEADS):
        maps = []
        for mp in range(2):
            d = feat[_F_DQ + (2 * hd + mp) * DIFF_QK:_F_DQ + (2 * hd + mp + 1) * DIFF_QK]
            x1, x2 = d[:rh], d[rh:2 * rh]
            q = jnp.concatenate([x1 * cbt - x2 * sbt, x2 * cbt + x1 * sbt, d[2 * rh:]], axis=0)
            maps.append((q * DIFF_QSCALE).astype(BF16))
        sl = slice(hd * LANES, (hd + 1) * LANES)
        dq1t_ref[sl, :] = jnp.concatenate([maps[0], zmap], axis=0)
        dq2t_ref[sl, :] = jnp.concatenate([zmap, maps[1]], axis=0)
    for c in range(PROJ_TILE // KV_TILE):
        dvt_ref[c] = feat[_F_DV:, c * KV_TILE:(c + 1) * KV_TILE].astype(BF16)

    c = tok[:, _T_CKV:_T_CKV + KV_RANK]
    c = c * lax.rsqrt(jnp.mean(c * c, axis=-1, keepdims=True) + NORM_EPS) * kvg_ref[...]
    c = c.astype(BF16)
    kn = jnp.dot(c, wuk_ref[...], preferred_element_type=F32)
    vt = lax.dot_general(wuvt_ref[...], c, _NT, preferred_element_type=F32)
    for ci in range(PROJ_TILE // KV_TILE):
        vat_ref[ci] = vt[:, ci * KV_TILE:(ci + 1) * KV_TILE].astype(BF16)
    kr = _rope_lanes(tok[:, _T_KR:_T_KR + LANES], ca_ref[...], sau_ref[...], sad_ref[...], half)
    kr = kr.astype(BF16)
    for hd in range(MLA_HEADS):
        base = hd * MLA_QK_PAD
        ka_ref[:, base:base + LANES] = kn[:, hd * LANES:(hd + 1) * LANES].astype(BF16)
        ka_ref[:, base + LANES:base + 2 * LANES] = kr
    ga_ref[...] = _silu(tok[:, _T_GA:_T_GA + 512]).astype(BF16)
    cb, sbu, sbd = cb_ref[...], sbu_ref[...], sbd_ref[...]
    for hd in range(DIFF_HEADS):
        sl = slice(_T_DK + hd * LANES, _T_DK + (hd + 1) * LANES)
        dk_ref[:, hd * LANES:(hd + 1) * LANES] = _rope_lanes(tok[:, sl], cb, sbu, sbd, rh).astype(BF16)
    gb_ref[...] = _silu(tok[:, _T_GB:_T_GB + 512]).astype(BF16)


def _softmax_pv(s_t, vt, m_ref, acc_ref, idx, lanes):
    m_old = m_ref[idx, :, lanes]
    m_new = jnp.maximum(m_old, jnp.max(s_t, axis=0, keepdims=True))
    alpha = jnp.exp2(m_old - m_new)
    p_t = jnp.exp2(s_t - m_new).astype(BF16)
    pv = jnp.dot(vt, p_t, preferred_element_type=F32)
    acc_ref[idx, :, lanes] = alpha * acc_ref[idx, :, lanes] + pv
    m_ref[idx, :, lanes] = m_new


def _run_pipelined(units, ready, following):
    pending = list(ready)
    stream = [scores for scores, _ in units] + [scores for scores, _ in following]
    for n, (_, update) in enumerate(units):
        if n + SCORE_LOOKAHEAD < len(stream):
            pending.append(stream[n + SCORE_LOOKAHEAD]())
        update(pending.pop(0))
    return pending


def _attn_kernel(qat_ref, dq1t_ref, dq2t_ref, ga_ref, gb_ref, x_ref,
                 ka_ref, vat_ref, dk_ref, dvt_ref, wout_ref,
                 lq1_ref, lk1_ref, lq2_ref, lk2_ref, subg_ref, postg_ref,
                 out_ref, m_ref, acc_ref, carry_ref, *, lambda_init):
    qi = pl.program_id(1)
    m_ref[...] = jnp.full(m_ref.shape, -jnp.inf, F32)
    acc_ref[...] = jnp.zeros(acc_ref.shape, F32)

    ones = jnp.ones((BF16_ROWS, KV_TILE), BF16)
    key = lax.broadcasted_iota(jnp.int32, (KV_TILE, Q_TILE), 0)
    qry = lax.broadcasted_iota(jnp.int32, (KV_TILE, Q_TILE), 1)
    causal = key <= qry
    all_q = slice(0, Q_TILE)
    upper_q = slice(KV_TILE, Q_TILE)

    def tile_units(j, lanes, mask):
        rows = pl.ds(pl.multiple_of(j * KV_TILE, KV_TILE), KV_TILE)
        units = []
        for u in range(N_MAPS):
            if u < MLA_HEADS:
                cs = slice(u * MLA_QK_PAD, (u + 1) * MLA_QK_PAD)
                vs = slice(u * MLA_V, (u + 1) * MLA_V)
                k_ref, qt_ref, v_ref = ka_ref, qat_ref, vat_ref
            else:
                d = u - MLA_HEADS
                cs = vs = slice((d // 2) * LANES, (d // 2 + 1) * LANES)
                k_ref, qt_ref, v_ref = dk_ref, (dq2t_ref if d % 2 else dq1t_ref), dvt_ref

            def scores(cs=cs, k_ref=k_ref, qt_ref=qt_ref):
                return jnp.dot(k_ref[rows, cs], qt_ref[cs, lanes], preferred_element_type=F32)

            def update(s_t, u=u, vs=vs, v_ref=v_ref):
                if mask is not None:
                    s_t = jnp.where(mask, s_t, -jnp.inf)
                vt = jnp.concatenate([v_ref[j, vs, :], ones], axis=0)
                _softmax_pv(s_t, vt, m_ref, acc_ref, u, lanes)

            units.append((scores, update))
        return units

    def load_carry():
        return [carry_ref[n] for n in range(SCORE_LOOKAHEAD)]

    def store_carry(scores):
        for n, s_t in enumerate(scores):
            carry_ref[n] = s_t

    store_carry([scores() for scores, _ in tile_units(0, all_q, None)[:SCORE_LOOKAHEAD]])

    def body(i, carry):
        units = tile_units(2 * i, all_q, None) + tile_units(2 * i + 1, all_q, None)
        store_carry(_run_pipelined(units, load_carry(), tile_units(2 * i + 2, all_q, None)))
        return carry

    lax.fori_loop(0, qi, body, 0)
    _run_pipelined(tile_units(2 * qi, all_q, causal)
                   + tile_units(2 * qi + 1, upper_q, causal[:, :KV_TILE]), load_carry(), [])

    lam = (jnp.exp(jnp.sum(lq1_ref[...] * lk1_ref[...], axis=1, keepdims=True))
           - jnp.exp(jnp.sum(lq2_ref[...] * lk2_ref[...], axis=1, keepdims=True))
           + lambda_init)

    def normalised(u):
        acc = acc_ref[u]
        return acc[:MLA_V] * (1.0 / acc[MLA_V:MLA_V + 1])

    oa = jnp.concatenate([normalised(h).T for h in range(MLA_HEADS)], axis=1)
    oa = oa * ga_ref[...].astype(F32)
    ob = []
    for h in range(DIFF_HEADS):
        o_t = normalised(MLA_HEADS + 2 * h) - lam * normalised(MLA_HEADS + 2 * h + 1)
        o_t = o_t * lax.rsqrt(jnp.mean(o_t * o_t, axis=0, keepdims=True) + SUBLN_EPS)
        ob.append(o_t.T * subg_ref[...] * (1.0 - lambda_init))
    ob = jnp.concatenate(ob, axis=1) * gb_ref[...].astype(F32)

    mixed_in = jnp.concatenate([oa, ob], axis=1).astype(BF16)
    mixed = jnp.dot(mixed_in, wout_ref[...], preferred_element_type=F32)
    y = mixed * lax.rsqrt(jnp.mean(mixed * mixed, axis=-1, keepdims=True) + NORM_EPS)
    out_ref[...] = x_ref[...] + y * postg_ref[...]


def _layer(x3, ln_pre_g, w_in, kv_norm_g, w_uk, w_uv, lq1, lk1, lq2, lk2,
           subln_g, w_out, ln_post_g, tables, lambda_init):
    batch, seq, d_model = x3.shape
    w_tok, w_feat = _split_weights(w_in)
    g2 = ln_pre_g.reshape(1, d_model)
    kvg2 = kv_norm_g.reshape(1, KV_RANK)
    wuk = w_uk.astype(BF16)
    wuvt = w_uv.T.astype(BF16)

    seq_tiles = seq // PROJ_TILE
    kv_per_tile = PROJ_TILE // KV_TILE
    n_kv = seq // KV_TILE
    tok_blk = lambda w: pl.BlockSpec((None, PROJ_TILE, w), lambda b, t: (b, t, 0))
    feat_blk = lambda r: pl.BlockSpec((None, r, PROJ_TILE), lambda b, t: (b, 0, t))
    vt_blk = pl.BlockSpec((None, kv_per_tile, 512, KV_TILE), lambda b, t: (b, t, 0, 0))
    full = lambda a: pl.BlockSpec(a.shape, lambda b, t: (0,) * a.ndim)
    tab_tok = pl.BlockSpec((PROJ_TILE, LANES), lambda b, t: (t, 0))
    tab_feat = lambda a: pl.BlockSpec((a.shape[0], PROJ_TILE), lambda b, t: (0, t))
    tok_tabs, feat_tabs = tables[:6], tables[6:]
    sds = jax.ShapeDtypeStruct
    qat, dq1t, dq2t, vat, dvt, ka, dk, ga, gb = pl.pallas_call(
        _proj_kernel,
        grid=(batch, seq_tiles),
        in_specs=[tok_blk(d_model), full(g2), full(w_tok), full(w_feat), full(kvg2),
                  full(wuk), full(wuvt)] + [tab_tok] * 6 + [tab_feat(a) for a in feat_tabs],
        out_specs=[feat_blk(1024), feat_blk(512), feat_blk(512), vt_blk, vt_blk,
                   tok_blk(1024), tok_blk(512), tok_blk(512), tok_blk(512)],
        out_shape=[sds((batch, 1024, seq), BF16), sds((batch, 512, seq), BF16),
                   sds((batch, 512, seq), BF16),
                   sds((batch, n_kv, 512, KV_TILE), BF16), sds((batch, n_kv, 512, KV_TILE), BF16),
                   sds((batch, seq, 1024), BF16), sds((batch, seq, 512), BF16),
                   sds((batch, seq, 512), BF16), sds((batch, seq, 512), BF16)],
        compiler_params=pltpu.CompilerParams(
            dimension_semantics=("arbitrary", "arbitrary"), vmem_limit_bytes=VMEM_LIMIT_BYTES),
        name="proj",
    )(x3, g2, w_tok, w_feat, kvg2, wuk, wuvt, *tok_tabs, *feat_tabs)

    q_tok = lambda w: pl.BlockSpec((None, Q_TILE, w), lambda b, i: (b, i, 0))
    q_feat = lambda r: pl.BlockSpec((None, r, Q_TILE), lambda b, i: (b, 0, i))
    kv_tok = lambda w: pl.BlockSpec((None, seq, w), lambda b, i: (b, 0, 0))
    kv_feat = pl.BlockSpec((None, n_kv, 512, KV_TILE), lambda b, i: (b, 0, 0, 0))
    fullb = lambda a: pl.BlockSpec(a.shape, lambda b, i: (0,) * a.ndim)
    wout = w_out.astype(BF16)
    lvecs = [v.reshape(1, DIFF_QK) for v in (lq1, lk1, lq2, lk2)]
    subg2 = subln_g.reshape(1, DIFF_V)
    postg2 = ln_post_g.reshape(1, d_model)
    return pl.pallas_call(
        functools.partial(_attn_kernel, lambda_init=lambda_init),
        grid=(batch, seq // Q_TILE),
        in_specs=[q_feat(1024), q_feat(512), q_feat(512), q_tok(512), q_tok(512), q_tok(d_model),
                  kv_tok(1024), kv_feat, kv_tok(512), kv_feat, fullb(wout)]
                 + [fullb(v) for v in lvecs] + [fullb(subg2), fullb(postg2)],
        out_specs=q_tok(d_model),
        out_shape=sds((batch, seq, d_model), F32),
        scratch_shapes=[
            pltpu.VMEM((N_MAPS, 1, Q_TILE), F32),
            pltpu.VMEM((N_MAPS, ACC_ROWS, Q_TILE), F32),
            pltpu.VMEM((SCORE_LOOKAHEAD, KV_TILE, Q_TILE), F32),
        ],
        compiler_params=pltpu.CompilerParams(
            dimension_semantics=("arbitrary", "arbitrary"), vmem_limit_bytes=VMEM_LIMIT_BYTES),
        name="attn",
    )(qat, dq1t, dq2t, ga, gb, x3, ka, vat, dk, dvt, wout, *lvecs, subg2, postg2)


def kernel(x, ln_pre_g, w_in, kv_norm_g, w_uk, w_uv, lambda_q1, lambda_k1,
           lambda_q2, lambda_k2, subln_g, w_out, ln_post_g):
    batch, seq, d_model = x.shape
    depth = w_in.shape[0]
    assert seq % PROJ_TILE == 0 and PROJ_TILE % KV_TILE == 0 and seq % Q_TILE == 0
    assert w_in.shape[2] == _C_END and d_model == 1024
    tables = _rope_tables(seq)
    for l in range(depth):
        lambda_init = 0.8 - 0.6 * math.exp(-0.3 * l)
        x = _layer(x, ln_pre_g[l], w_in[l], kv_norm_g[l], w_uk[l], w_uv[l],
                   lambda_q1[l], lambda_k1[l], lambda_q2[l], lambda_k2[l], subln_g[l],
                   w_out[l], ln_post_g[l], tables, lambda_init)
    return x
```

```python
import functools
import math

import jax
import jax.numpy as jnp
import numpy as np
from jax import lax
from jax.experimental import pallas as pl
from jax.experimental.pallas import tpu as pltpu

ROPE_THETA = 500000.0
NORM_EPS = 1e-6
SUBLN_EPS = 1e-5

MLA_HEADS = 4
MLA_NOPE = 128
MLA_ROPE = 64
MLA_V = 128
KV_RANK = 128
DIFF_HEADS = 4
DIFF_QK = 64
DIFF_V = 2 * DIFF_QK
DIFF_ROPE = DIFF_QK // 4

LANES = 128
BF16_ROWS = 16
MLA_QK_PAD = 2 * LANES
LOG2E = math.log2(math.e)
MLA_QSCALE = LOG2E / math.sqrt(MLA_NOPE + MLA_ROPE)
DIFF_QSCALE = LOG2E / math.sqrt(DIFF_QK)

PROJ_TILE = 512
KV_TILE = 256
Q_TILE = 2 * KV_TILE
N_MAPS = MLA_HEADS + 2 * DIFF_HEADS
ACC_ROWS = MLA_V + BF16_ROWS
SCORE_LOOKAHEAD = 2
VMEM_LIMIT_BYTES = 56 * 1024 * 1024

BF16 = jnp.bfloat16
F32 = jnp.float32
_NT = (((1,), (1,)), ((), ()))

_C_QN = 0
_C_QR = _C_QN + MLA_HEADS * MLA_NOPE
_C_CKV = _C_QR + MLA_HEADS * MLA_ROPE
_C_KR = _C_CKV + KV_RANK
_C_GA = _C_KR + MLA_ROPE
_C_DQ = _C_GA + MLA_HEADS * MLA_V
_C_DK = _C_DQ + DIFF_HEADS * 2 * DIFF_QK
_C_DV = _C_DK + DIFF_HEADS * 2 * DIFF_QK
_C_GB = _C_DV + DIFF_HEADS * DIFF_V
_C_END = _C_GB + DIFF_HEADS * DIFF_V
_P_PAD = LANES - MLA_ROPE
_P_QN, _P_QR, _P_CKV, _P_KR = _C_QN, _C_QR, _C_CKV, _C_KR
_P_GA, _P_DQ, _P_DK, _P_DV, _P_GB, _P_END = (c + _P_PAD for c in
                                             (_C_GA, _C_DQ, _C_DK, _C_DV, _C_GB, _C_END))
_T_CKV, _T_KR, _T_GA, _T_DK, _T_GB = 0, 128, 256, 768, 1280
_F_QN, _F_QR, _F_DQ, _F_DV = 0, 512, 768, 1280


def _pad_weights(w_in):
    pad = jnp.zeros((w_in.shape[0], _P_PAD), w_in.dtype)
    return jnp.concatenate([w_in[:, :_C_GA], pad, w_in[:, _C_GA:]], axis=1).astype(BF16)


def _rope_tables(seq):
    def tab(dim):
        inv_freq = ROPE_THETA ** (-np.arange(0, dim, 2, dtype=np.float64) / dim)
        ang = np.arange(seq, dtype=np.float64)[:, None] * inv_freq[None, :]
        return np.cos(ang), np.sin(ang)

    def lane_tables(c, s, group, rest_cos):
        half = c.shape[1]
        rest = np.full((seq, group - 2 * half), rest_cos)
        zero = np.zeros((seq, group - 2 * half))
        zh = np.zeros((seq, half))
        reps = LANES // group
        return (np.concatenate([c, c, rest] * reps, axis=1),
                np.concatenate([-s, zh, zero] * reps, axis=1),
                np.concatenate([zh, s, zero] * reps, axis=1))

    ca, sa = tab(MLA_ROPE)
    cb, sb = tab(DIFF_ROPE)
    tables = lane_tables(ca, sa, LANES, 0.0)
    tables += lane_tables(cb, sb, DIFF_QK, 1.0)
    tables += (ca.T, sa.T, cb.T, sb.T)
    return tuple(jnp.asarray(t, F32) for t in tables)


def _silu(g):
    return g / (1.0 + jnp.exp(-g))


def _rope_lanes(v, cos, sin_up, sin_down, half):
    return (v * cos + pltpu.roll(v, LANES - half, axis=1) * sin_up
            + pltpu.roll(v, half, axis=1) * sin_down)


def _proj_kernel(x_ref, g_ref, w_ref, kvg_ref, wuk_ref, wuvt_ref,
                 ca_ref, sau_ref, sad_ref, cb_ref, sbu_ref, sbd_ref,
                 cat_ref, sat_ref, cbt_ref, sbt_ref,
                 qat_ref, dq1t_ref, dq2t_ref, vat_ref, dvt_ref, ka_ref, dk_ref, ga_ref, gb_ref):
    x = x_ref[...]
    ms = jnp.mean(x * x, axis=-1, keepdims=True)
    h = (x * lax.rsqrt(ms + NORM_EPS) * g_ref[...]).astype(BF16)

    def proj(lo, hi):
        return jnp.dot(h, w_ref[:, lo:hi], preferred_element_type=F32)

    tok = jnp.concatenate([proj(_P_CKV, _P_DQ), proj(_P_DK, _P_DV), proj(_P_GB, _P_END)], axis=1)
    feat = jnp.concatenate([proj(_P_QN, _P_CKV).T, proj(_P_DQ, _P_DK).T, proj(_P_DV, _P_GB).T],
                           axis=0)

    cat, sat = cat_ref[...], sat_ref[...]
    half = MLA_ROPE // 2
    zpad = jnp.zeros((MLA_QK_PAD - MLA_NOPE - MLA_ROPE, PROJ_TILE), BF16)
    for hd in range(MLA_HEADS):
        base = hd * MLA_QK_PAD
        qn = feat[_F_QN + hd * MLA_NOPE:_F_QN + (hd + 1) * MLA_NOPE]
        qat_ref[base:base + MLA_NOPE, :] = (qn * MLA_QSCALE).astype(BF16)
        x1 = feat[_F_QR + hd * MLA_ROPE:_F_QR + hd * MLA_ROPE + half]
        x2 = feat[_F_QR + hd * MLA_ROPE + half:_F_QR + (hd + 1) * MLA_ROPE]
        r1 = (x1 * cat - x2 * sat) * MLA_QSCALE
        r2 = (x2 * cat + x1 * sat) * MLA_QSCALE
        rope_rows = jnp.concatenate([r1.astype(BF16), r2.astype(BF16), zpad], axis=0)
        qat_ref[base + MLA_NOPE:base + MLA_QK_PAD, :] = rope_rows

    cbt, sbt = cbt_ref[...], sbt_ref[...]
    rh = DIFF_ROPE // 2
    zmap = jnp.zeros((DIFF_QK, PROJ_TILE), BF16)
    for hd in range(DIFF_HEADS):
        maps = []
        for mp in range(2):
            d = feat[_F_DQ + (2 * hd + mp) * DIFF_QK:_F_DQ + (2 * hd + mp + 1) * DIFF_QK]
            x1, x2 = d[:rh], d[rh:2 * rh]
            q = jnp.concatenate([x1 * cbt - x2 * sbt, x2 * cbt + x1 * sbt, d[2 * rh:]], axis=0)
            maps.append((q * DIFF_QSCALE).astype(BF16))
        sl = slice(hd * LANES, (hd + 1) * LANES)
        dq1t_ref[sl, :] = jnp.concatenate([maps[0], zmap], axis=0)
        dq2t_ref[sl, :] = jnp.concatenate([zmap, maps[1]], axis=0)
    for c in range(PROJ_TILE // KV_TILE):
        dvt_ref[c] = feat[_F_DV:, c * KV_TILE:(c + 1) * KV_TILE].astype(BF16)

    c = tok[:, _T_CKV:_T_CKV + KV_RANK]
    c = c * lax.rsqrt(jnp.mean(c * c, axis=-1, keepdims=True) + NORM_EPS) * kvg_ref[...]
    c = c.astype(BF16)
    kn = jnp.dot(c, wuk_ref[...], preferred_element_type=F32)
    vt = lax.dot_general(wuvt_ref[...], c, _NT, preferred_element_type=F32)
    for ci in range(PROJ_TILE // KV_TILE):
        vat_ref[ci] = vt[:, ci * KV_TILE:(ci + 1) * KV_TILE].astype(BF16)
    kr = _rope_lanes(tok[:, _T_KR:_T_KR + LANES], ca_ref[...], sau_ref[...], sad_ref[...], half)
    kr = kr.astype(BF16)
    for hd in range(MLA_HEADS):
        base = hd * MLA_QK_PAD
        ka_ref[:, base:base + LANES] = kn[:, hd * LANES:(hd + 1) * LANES].astype(BF16)
        ka_ref[:, base + LANES:base + 2 * LANES] = kr
    ga_ref[...] = _silu(tok[:, _T_GA:_T_GA + 512]).astype(BF16)
    cb, sbu, sbd = cb_ref[...], sbu_ref[...], sbd_ref[...]
    for hd in range(DIFF_HEADS):
        sl = slice(_T_DK + hd * LANES, _T_DK + (hd + 1) * LANES)
        dk_ref[:, hd * LANES:(hd + 1) * LANES] = _rope_lanes(tok[:, sl], cb, sbu, sbd, rh).astype(BF16)
    gb_ref[...] = _silu(tok[:, _T_GB:_T_GB + 512]).astype(BF16)


def _softmax_pv(s_t, vt, m_ref, acc_ref, idx, lanes):
    m_old = m_ref[idx, :, lanes]
    m_new = jnp.maximum(m_old, jnp.max(s_t, axis=0, keepdims=True))
    alpha = jnp.exp2(m_old - m_new)
    p_t = jnp.exp2(s_t - m_new).astype(BF16)
    pv = jnp.dot(vt, p_t, preferred_element_type=F32)
    acc_ref[idx, :, lanes] = alpha * acc_ref[idx, :, lanes] + pv
    m_ref[idx, :, lanes] = m_new


def _run_pipelined(units, ready, following):
    pending = list(ready)
    stream = [scores for scores, _ in units] + [scores for scores, _ in following]
    for n, (_, update) in enumerate(units):
        if n + SCORE_LOOKAHEAD < len(stream):
            pending.append(stream[n + SCORE_LOOKAHEAD]())
        update(pending.pop(0))
    return pending


def _attn_kernel(qat_ref, dq1t_ref, dq2t_ref, ga_ref, gb_ref, x_ref,
                 ka_ref, vat_ref, dk_ref, dvt_ref, wout_ref,
                 lq1_ref, lk1_ref, lq2_ref, lk2_ref, subg_ref, postg_ref,
                 out_ref, m_ref, acc_ref, carry_ref, *, lambda_init):
    qi = pl.program_id(1)
    m_ref[...] = jnp.full(m_ref.shape, -jnp.inf, F32)
    acc_ref[...] = jnp.zeros(acc_ref.shape, F32)

    ones = jnp.ones((BF16_ROWS, KV_TILE), BF16)
    key = lax.broadcasted_iota(jnp.int32, (KV_TILE, Q_TILE), 0)
    qry = lax.broadcasted_iota(jnp.int32, (KV_TILE, Q_TILE), 1)
    causal = key <= qry
    all_q = slice(0, Q_TILE)
    upper_q = slice(KV_TILE, Q_TILE)

    def tile_units(j, lanes, mask):
        rows = pl.ds(pl.multiple_of(j * KV_TILE, KV_TILE), KV_TILE)
        units = []
        for u in range(N_MAPS):
            if u < MLA_HEADS:
                cs = slice(u * MLA_QK_PAD, (u + 1) * MLA_QK_PAD)
                vs = slice(u * MLA_V, (u + 1) * MLA_V)
                k_ref, qt_ref, v_ref = ka_ref, qat_ref, vat_ref
            else:
                d = u - MLA_HEADS
                cs = vs = slice((d // 2) * LANES, (d // 2 + 1) * LANES)
                k_ref, qt_ref, v_ref = dk_ref, (dq2t_ref if d % 2 else dq1t_ref), dvt_ref

            def scores(cs=cs, k_ref=k_ref, qt_ref=qt_ref):
                return jnp.dot(k_ref[rows, cs], qt_ref[cs, lanes], preferred_element_type=F32)

            def update(s_t, u=u, vs=vs, v_ref=v_ref):
                if mask is not None:
                    s_t = jnp.where(mask, s_t, -jnp.inf)
                vt = jnp.concatenate([v_ref[j, vs, :], ones], axis=0)
                _softmax_pv(s_t, vt, m_ref, acc_ref, u, lanes)

            units.append((scores, update))
        return units

    def load_carry():
        return [carry_ref[n] for n in range(SCORE_LOOKAHEAD)]

    def store_carry(scores):
        for n, s_t in enumerate(scores):
            carry_ref[n] = s_t

    store_carry([scores() for scores, _ in tile_units(0, all_q, None)[:SCORE_LOOKAHEAD]])

    def body(i, carry):
        units = tile_units(2 * i, all_q, None) + tile_units(2 * i + 1, all_q, None)
        store_carry(_run_pipelined(units, load_carry(), tile_units(2 * i + 2, all_q, None)))
        return carry

    lax.fori_loop(0, qi, body, 0)
    _run_pipelined(tile_units(2 * qi, all_q, causal)
                   + tile_units(2 * qi + 1, upper_q, causal[:, :KV_TILE]), load_carry(), [])

    lam = (jnp.exp(jnp.sum(lq1_ref[...] * lk1_ref[...], axis=1, keepdims=True))
           - jnp.exp(jnp.sum(lq2_ref[...] * lk2_ref[...], axis=1, keepdims=True))
           + lambda_init)

    def normalised(u):
        acc = acc_ref[u]
        return acc[:MLA_V] * (1.0 / acc[MLA_V:MLA_V + 1])

    oa = jnp.concatenate([normalised(h).T for h in range(MLA_HEADS)], axis=1)
    oa = oa * ga_ref[...].astype(F32)
    ob = []
    for h in range(DIFF_HEADS):
        o_t = normalised(MLA_HEADS + 2 * h) - lam * normalised(MLA_HEADS + 2 * h + 1)
        o_t = o_t * lax.rsqrt(jnp.mean(o_t * o_t, axis=0, keepdims=True) + SUBLN_EPS)
        ob.append(o_t.T * subg_ref[...] * (1.0 - lambda_init))
    ob = jnp.concatenate(ob, axis=1) * gb_ref[...].astype(F32)

    mixed_in = jnp.concatenate([oa, ob], axis=1).astype(BF16)
    mixed = jnp.dot(mixed_in, wout_ref[...], preferred_element_type=F32)
    y = mixed * lax.rsqrt(jnp.mean(mixed * mixed, axis=-1, keepdims=True) + NORM_EPS)
    out_ref[...] = x_ref[...] + y * postg_ref[...]


def _layer(x3, ln_pre_g, w_in, kv_norm_g, w_uk, w_uv, lq1, lk1, lq2, lk2,
           subln_g, w_out, ln_post_g, tables, lambda_init):
    batch, seq, d_model = x3.shape
    w_pad = _pad_weights(w_in)
    g2 = ln_pre_g.reshape(1, d_model)
    kvg2 = kv_norm_g.reshape(1, KV_RANK)
    wuk = w_uk.astype(BF16)
    wuvt = w_uv.T.astype(BF16)

    seq_tiles = seq // PROJ_TILE
    kv_per_tile = PROJ_TILE // KV_TILE
    n_kv = seq // KV_TILE
    tok_blk = lambda w: pl.BlockSpec((None, PROJ_TILE, w), lambda b, t: (b, t, 0))
    feat_blk = lambda r: pl.BlockSpec((None, r, PROJ_TILE), lambda b, t: (b, 0, t))
    vt_blk = pl.BlockSpec((None, kv_per_tile, 512, KV_TILE), lambda b, t: (b, t, 0, 0))
    full = lambda a: pl.BlockSpec(a.shape, lambda b, t: (0,) * a.ndim)
    tab_tok = pl.BlockSpec((PROJ_TILE, LANES), lambda b, t: (t, 0))
    tab_feat = lambda a: pl.BlockSpec((a.shape[0], PROJ_TILE), lambda b, t: (0, t))
    tok_tabs, feat_tabs = tables[:6], tables[6:]
    sds = jax.ShapeDtypeStruct
    qat, dq1t, dq2t, vat, dvt, ka, dk, ga, gb = pl.pallas_call(
        _proj_kernel,
        grid=(batch, seq_tiles),
        in_specs=[tok_blk(d_model), full(g2), full(w_pad), full(kvg2),
                  full(wuk), full(wuvt)] + [tab_tok] * 6 + [tab_feat(a) for a in feat_tabs],
        out_specs=[feat_blk(1024), feat_blk(512), feat_blk(512), vt_blk, vt_blk,
                   tok_blk(1024), tok_blk(512), tok_blk(512), tok_blk(512)],
        out_shape=[sds((batch, 1024, seq), BF16), sds((batch, 512, seq), BF16),
                   sds((batch, 512, seq), BF16),
                   sds((batch, n_kv, 512, KV_TILE), BF16), sds((batch, n_kv, 512, KV_TILE), BF16),
                   sds((batch, seq, 1024), BF16), sds((batch, seq, 512), BF16),
                   sds((batch, seq, 512), BF16), sds((batch, seq, 512), BF16)],
        compiler_params=pltpu.CompilerParams(
            dimension_semantics=("arbitrary", "arbitrary"), vmem_limit_bytes=VMEM_LIMIT_BYTES),
        name="proj",
    )(x3, g2, w_pad, kvg2, wuk, wuvt, *tok_tabs, *feat_tabs)

    q_tok = lambda w: pl.BlockSpec((None, Q_TILE, w), lambda b, i: (b, i, 0))
    q_feat = lambda r: pl.BlockSpec((None, r, Q_TILE), lambda b, i: (b, 0, i))
    kv_tok = lambda w: pl.BlockSpec((None, seq, w), lambda b, i: (b, 0, 0))
    kv_feat = pl.BlockSpec((None, n_kv, 512, KV_TILE), lambda b, i: (b, 0, 0, 0))
    fullb = lambda a: pl.BlockSpec(a.shape, lambda b, i: (0,) * a.ndim)
    wout = w_out.astype(BF16)
    lvecs = [v.reshape(1, DIFF_QK) for v in (lq1, lk1, lq2, lk2)]
    subg2 = subln_g.reshape(1, DIFF_V)
    postg2 = ln_post_g.reshape(1, d_model)
    return pl.pallas_call(
        functools.partial(_attn_kernel, lambda_init=lambda_init),
        grid=(batch, seq // Q_TILE),
        in_specs=[q_feat(1024), q_feat(512), q_feat(512), q_tok(512), q_tok(512), q_tok(d_model),
                  kv_tok(1024), kv_feat, kv_tok(512), kv_feat, fullb(wout)]
                 + [fullb(v) for v in lvecs] + [fullb(subg2), fullb(postg2)],
        out_specs=q_tok(d_model),
        out_shape=sds((batch, seq, d_model), F32),
        scratch_shapes=[
            pltpu.VMEM((N_MAPS, 1, Q_TILE), F32),
            pltpu.VMEM((N_MAPS, ACC_ROWS, Q_TILE), F32),
            pltpu.VMEM((SCORE_LOOKAHEAD, KV_TILE, Q_TILE), F32),
        ],
        compiler_params=pltpu.CompilerParams(
            dimension_semantics=("arbitrary", "arbitrary"), vmem_limit_bytes=VMEM_LIMIT_BYTES),
        name="attn",
    )(qat, dq1t, dq2t, ga, gb, x3, ka, vat, dk, dvt, wout, *lvecs, subg2, postg2)


def kernel(x, ln_pre_g, w_in, kv_norm_g, w_uk, w_uv, lambda_q1, lambda_k1,
           lambda_q2, lambda_k2, subln_g, w_out, ln_post_g):
    batch, seq, d_model = x.shape
    depth = w_in.shape[0]
    assert seq % PROJ_TILE == 0 and PROJ_TILE % KV_TILE == 0 and seq % Q_TILE == 0
    assert w_in.shape[2] == _C_END and d_model == 1024
    tables = _rope_tables(seq)
    for l in range(depth):
        lambda_init = 0.8 - 0.6 * math.exp(-0.3 * l)
        x = _layer(x, ln_pre_g[l], w_in[l], kv_norm_g[l], w_uk[l], w_uv[l],
                   lambda_q1[l], lambda_k1[l], lambda_q2[l], lambda_k2[l], subln_g[l],
                   w_out[l], ln_post_g[l], tables, lambda_init)
    return x
```

```python
import functools
import math

import jax
import jax.numpy as jnp
import numpy as np
from jax import lax
from jax.experimental import pallas as pl
from jax.experimental.pallas import tpu as pltpu

ROPE_THETA = 500000.0
NORM_EPS = 1e-6
SUBLN_EPS = 1e-5

MLA_HEADS = 4
MLA_NOPE = 128
MLA_ROPE = 64
MLA_V = 128
KV_RANK = 128
DIFF_HEADS = 4
DIFF_QK = 64
DIFF_V = 2 * DIFF_QK
DIFF_ROPE = DIFF_QK // 4

LANES = 128
BF16_ROWS = 16
MLA_QK_PAD = 2 * LANES
LOG2E = math.log2(math.e)
MLA_QSCALE = LOG2E / math.sqrt(MLA_NOPE + MLA_ROPE)
DIFF_QSCALE = LOG2E / math.sqrt(DIFF_QK)

PROJ_TILE = 512
KV_TILE = 256
Q_TILE = 2 * KV_TILE
N_MAPS = MLA_HEADS + 2 * DIFF_HEADS
ACC_ROWS = MLA_V + BF16_ROWS
SCORE_LOOKAHEAD = 2
VMEM_LIMIT_BYTES = 56 * 1024 * 1024

BF16 = jnp.bfloat16
F32 = jnp.float32
_NT = (((1,), (1,)), ((), ()))

_C_QN = 0
_C_QR = _C_QN + MLA_HEADS * MLA_NOPE
_C_CKV = _C_QR + MLA_HEADS * MLA_ROPE
_C_KR = _C_CKV + KV_RANK
_C_GA = _C_KR + MLA_ROPE
_C_DQ = _C_GA + MLA_HEADS * MLA_V
_C_DK = _C_DQ + DIFF_HEADS * 2 * DIFF_QK
_C_DV = _C_DK + DIFF_HEADS * 2 * DIFF_QK
_C_GB = _C_DV + DIFF_HEADS * DIFF_V
_C_END = _C_GB + DIFF_HEADS * DIFF_V
PROJ_COLS = -(-_C_END // LANES) * LANES
PROJ_BLOCK_RUNS = ((6, 12), (12, 20), (20, 28), (0, 6))


def _rope_tables(seq):
    def tab(dim):
        inv_freq = ROPE_THETA ** (-np.arange(0, dim, 2, dtype=np.float64) / dim)
        ang = np.arange(seq, dtype=np.float64)[:, None] * inv_freq[None, :]
        return np.cos(ang), np.sin(ang)

    def lane_tables(c, s, group, rest_cos):
        half = c.shape[1]
        rest = np.full((seq, group - 2 * half), rest_cos)
        zero = np.zeros((seq, group - 2 * half))
        zh = np.zeros((seq, half))
        reps = LANES // group
        return (np.concatenate([c, c, rest] * reps, axis=1),
                np.concatenate([-s, zh, zero] * reps, axis=1),
                np.concatenate([zh, s, zero] * reps, axis=1))

    ca, sa = tab(MLA_ROPE)
    cb, sb = tab(DIFF_ROPE)
    tables = lane_tables(ca, sa, LANES, 0.0)
    tables += lane_tables(cb, sb, DIFF_QK, 1.0)
    tables += (ca.T, sa.T, cb.T, sb.T)
    return tuple(jnp.asarray(t, F32) for t in tables)


def _silu(g):
    return g / (1.0 + jnp.exp(-g))


def _rope_lanes(v, cos, sin_up, sin_down, half):
    return (v * cos + pltpu.roll(v, LANES - half, axis=1) * sin_up
            + pltpu.roll(v, half, axis=1) * sin_down)


def _proj_kernel(x_ref, g_ref, w_ref, kvg_ref, wuk_ref, wuvt_ref,
                 ca_ref, sau_ref, sad_ref, cb_ref, sbu_ref, sbd_ref,
                 cat_ref, sat_ref, cbt_ref, sbt_ref,
                 qat_ref, dq1t_ref, dq2t_ref, vat_ref, dvt_ref, ka_ref, dk_ref, ga_ref, gb_ref):
    x = x_ref[...]
    ms = jnp.mean(x * x, axis=-1, keepdims=True)
    h = (x * lax.rsqrt(ms + NORM_EPS) * g_ref[...]).astype(BF16)

    blocks = {}
    for lo_blk, hi_blk in PROJ_BLOCK_RUNS:
        run = jnp.dot(h, w_ref[:, lo_blk * LANES:hi_blk * LANES], preferred_element_type=F32)
        for b in range(lo_blk, hi_blk):
            blocks[b] = run[:, (b - lo_blk) * LANES:(b - lo_blk + 1) * LANES]
    half_lanes = lax.broadcasted_iota(jnp.int32, (1, LANES), 1) < LANES // 2

    def tok_cols(col, width):
        first = col // LANES
        if col % LANES == 0:
            return jnp.concatenate([blocks[first + j] for j in range(width // LANES)], axis=1)
        assert col % LANES == LANES // 2 and width % LANES == 0
        swapped = [pltpu.roll(blocks[first + j], LANES // 2, axis=1)
                   for j in range(width // LANES + 1)]
        return jnp.concatenate([jnp.where(half_lanes, swapped[j], swapped[j + 1])
                                for j in range(width // LANES)], axis=1)

    def feat_rows(col, width):
        lo = col // LANES
        hi = -(-(col + width) // LANES)
        rows = jnp.concatenate([blocks[b].T for b in range(lo, hi)], axis=0)
        return rows[col - lo * LANES:col - lo * LANES + width]

    feat_q = feat_rows(_C_QN, _C_CKV - _C_QN)
    feat_dq = feat_rows(_C_DQ, _C_DK - _C_DQ)
    feat_dv = feat_rows(_C_DV, _C_GB - _C_DV)

    cat, sat = cat_ref[...], sat_ref[...]
    half = MLA_ROPE // 2
    zpad = jnp.zeros((MLA_QK_PAD - MLA_NOPE - MLA_ROPE, PROJ_TILE), BF16)
    for hd in range(MLA_HEADS):
        base = hd * MLA_QK_PAD
        qn = feat_q[hd * MLA_NOPE:(hd + 1) * MLA_NOPE]
        qat_ref[base:base + MLA_NOPE, :] = (qn * MLA_QSCALE).astype(BF16)
        qr = feat_q[_C_QR + hd * MLA_ROPE:_C_QR + (hd + 1) * MLA_ROPE]
        x1, x2 = qr[:half], qr[half:]
        r1 = (x1 * cat - x2 * sat) * MLA_QSCALE
        r2 = (x2 * cat + x1 * sat) * MLA_QSCALE
        rope_rows = jnp.concatenate([r1.astype(BF16), r2.astype(BF16), zpad], axis=0)
        qat_ref[base + MLA_NOPE:base + MLA_QK_PAD, :] = rope_rows

    cbt, sbt = cbt_ref[...], sbt_ref[...]
    rh = DIFF_ROPE // 2
    zmap = jnp.zeros((DIFF_QK, PROJ_TILE), BF16)
    for hd in range(DIFF_HEADS):
        maps = []
        for mp in range(2):
            d = feat_dq[(2 * hd + mp) * DIFF_QK:(2 * hd + mp + 1) * DIFF_QK]
            x1, x2 = d[:rh], d[rh:2 * rh]
            q = jnp.concatenate([x1 * cbt - x2 * sbt, x2 * cbt + x1 * sbt, d[2 * rh:]], axis=0)
            maps.append((q * DIFF_QSCALE).astype(BF16))
        sl = slice(hd * LANES, (hd + 1) * LANES)
        dq1t_ref[sl, :] = jnp.concatenate([maps[0], zmap], axis=0)
        dq2t_ref[sl, :] = jnp.concatenate([zmap, maps[1]], axis=0)
    for c in range(PROJ_TILE // KV_TILE):
        dvt_ref[c] = feat_dv[:, c * KV_TILE:(c + 1) * KV_TILE].astype(BF16)

    c = tok_cols(_C_CKV, KV_RANK)
    c = c * lax.rsqrt(jnp.mean(c * c, axis=-1, keepdims=True) + NORM_EPS) * kvg_ref[...]
    c = c.astype(BF16)
    kn = jnp.dot(c, wuk_ref[...], preferred_element_type=F32)
    vt = lax.dot_general(wuvt_ref[...], c, _NT, preferred_element_type=F32)
    for ci in range(PROJ_TILE // KV_TILE):
        vat_ref[ci] = vt[:, ci * KV_TILE:(ci + 1) * KV_TILE].astype(BF16)
    kr = _rope_lanes(tok_cols(_C_KR, LANES), ca_ref[...], sau_ref[...], sad_ref[...], half)
    kr = kr.astype(BF16)
    for hd in range(MLA_HEADS):
        base = hd * MLA_QK_PAD
        ka_ref[:, base:base + LANES] = kn[:, hd * LANES:(hd + 1) * LANES].astype(BF16)
        ka_ref[:, base + LANES:base + 2 * LANES] = kr
    ga_ref[...] = _silu(tok_cols(_C_GA, _C_DQ - _C_GA)).astype(BF16)
    cb, sbu, sbd = cb_ref[...], sbu_ref[...], sbd_ref[...]
    dk_tok = tok_cols(_C_DK, _C_DV - _C_DK)
    for hd in range(DIFF_HEADS):
        sl = slice(hd * LANES, (hd + 1) * LANES)
        dk_ref[:, sl] = _rope_lanes(dk_tok[:, sl], cb, sbu, sbd, rh).astype(BF16)
    gb_ref[...] = _silu(tok_cols(_C_GB, _C_END - _C_GB)).astype(BF16)


def _softmax_pv(s_t, vt, m_ref, acc_ref, idx, lanes):
    m_old = m_ref[idx, :, lanes]
    m_new = jnp.maximum(m_old, jnp.max(s_t, axis=0, keepdims=True))
    alpha = jnp.exp2(m_old - m_new)
    p_t = jnp.exp2(s_t - m_new).astype(BF16)
    pv = jnp.dot(vt, p_t, preferred_element_type=F32)
    acc_ref[idx, :, lanes] = alpha * acc_ref[idx, :, lanes] + pv
    m_ref[idx, :, lanes] = m_new


def _run_pipelined(units, ready, following):
    pending = list(ready)
    stream = [scores for scores, _ in units] + [scores for scores, _ in following]
    for n, (_, update) in enumerate(units):
        if n + SCORE_LOOKAHEAD < len(stream):
            pending.append(stream[n + SCORE_LOOKAHEAD]())
        update(pending.pop(0))
    return pending


def _attn_kernel(qat_ref, dq1t_ref, dq2t_ref, ga_ref, gb_ref, x_ref,
                 ka_ref, vat_ref, dk_ref, dvt_ref, wout_ref,
                 lq1_ref, lk1_ref, lq2_ref, lk2_ref, subg_ref, postg_ref,
                 out_ref, m_ref, acc_ref, carry_ref, *, lambda_init):
    qi = pl.program_id(1)
    m_ref[...] = jnp.full(m_ref.shape, -jnp.inf, F32)
    acc_ref[...] = jnp.zeros(acc_ref.shape, F32)

    ones = jnp.ones((BF16_ROWS, KV_TILE), BF16)
    key = lax.broadcasted_iota(jnp.int32, (KV_TILE, Q_TILE), 0)
    qry = lax.broadcasted_iota(jnp.int32, (KV_TILE, Q_TILE), 1)
    causal = key <= qry
    all_q = slice(0, Q_TILE)
    upper_q = slice(KV_TILE, Q_TILE)

    def tile_units(j, lanes, mask):
        rows = pl.ds(pl.multiple_of(j * KV_TILE, KV_TILE), KV_TILE)
        units = []
        for u in range(N_MAPS):
            if u < MLA_HEADS:
                cs = slice(u * MLA_QK_PAD, (u + 1) * MLA_QK_PAD)
                vs = slice(u * MLA_V, (u + 1) * MLA_V)
                k_ref, qt_ref, v_ref = ka_ref, qat_ref, vat_ref
            else:
                d = u - MLA_HEADS
                cs = vs = slice((d // 2) * LANES, (d // 2 + 1) * LANES)
                k_ref, qt_ref, v_ref = dk_ref, (dq2t_ref if d % 2 else dq1t_ref), dvt_ref

            def scores(cs=cs, k_ref=k_ref, qt_ref=qt_ref):
                return jnp.dot(k_ref[rows, cs], qt_ref[cs, lanes], preferred_element_type=F32)

            def update(s_t, u=u, vs=vs, v_ref=v_ref):
                if mask is not None:
                    s_t = jnp.where(mask, s_t, -jnp.inf)
                vt = jnp.concatenate([v_ref[j, vs, :], ones], axis=0)
                _softmax_pv(s_t, vt, m_ref, acc_ref, u, lanes)

            units.append((scores, update))
        return units

    def load_carry():
        return [carry_ref[n] for n in range(SCORE_LOOKAHEAD)]

    def store_carry(scores):
        for n, s_t in enumerate(scores):
            carry_ref[n] = s_t

    store_carry([scores() for scores, _ in tile_units(0, all_q, None)[:SCORE_LOOKAHEAD]])

    def body(i, carry):
        units = tile_units(2 * i, all_q, None) + tile_units(2 * i + 1, all_q, None)
        store_carry(_run_pipelined(units, load_carry(), tile_units(2 * i + 2, all_q, None)))
        return carry

    lax.fori_loop(0, qi, body, 0)
    _run_pipelined(tile_units(2 * qi, all_q, causal)
                   + tile_units(2 * qi + 1, upper_q, causal[:, :KV_TILE]), load_carry(), [])

    lam = (jnp.exp(jnp.sum(lq1_ref[...] * lk1_ref[...], axis=1, keepdims=True))
           - jnp.exp(jnp.sum(lq2_ref[...] * lk2_ref[...], axis=1, keepdims=True))
           + lambda_init)

    def normalised(u):
        acc = acc_ref[u]
        return acc[:MLA_V] * (1.0 / acc[MLA_V:MLA_V + 1])

    oa = jnp.concatenate([normalised(h).T for h in range(MLA_HEADS)], axis=1)
    oa = oa * ga_ref[...].astype(F32)
    ob = []
    for h in range(DIFF_HEADS):
        o_t = normalised(MLA_HEADS + 2 * h) - lam * normalised(MLA_HEADS + 2 * h + 1)
        o_t = o_t * lax.rsqrt(jnp.mean(o_t * o_t, axis=0, keepdims=True) + SUBLN_EPS)
        ob.append(o_t.T * subg_ref[...] * (1.0 - lambda_init))
    ob = jnp.concatenate(ob, axis=1) * gb_ref[...].astype(F32)

    mixed_in = jnp.concatenate([oa, ob], axis=1).astype(BF16)
    mixed = jnp.dot(mixed_in, wout_ref[...], preferred_element_type=F32)
    y = mixed * lax.rsqrt(jnp.mean(mixed * mixed, axis=-1, keepdims=True) + NORM_EPS)
    out_ref[...] = x_ref[...] + y * postg_ref[...]


def _layer(x3, ln_pre_g, w_in, kv_norm_g, w_uk, w_uv, lq1, lk1, lq2, lk2,
           subln_g, w_out, ln_post_g, tables, lambda_init):
    batch, seq, d_model = x3.shape
    w_bf = jnp.pad(w_in, ((0, 0), (0, PROJ_COLS - _C_END))).astype(BF16)
    g2 = ln_pre_g.reshape(1, d_model)
    kvg2 = kv_norm_g.reshape(1, KV_RANK)
    wuk = w_uk.astype(BF16)
    wuvt = w_uv.T.astype(BF16)

    seq_tiles = seq // PROJ_TILE
    kv_per_tile = PROJ_TILE // KV_TILE
    n_kv = seq // KV_TILE
    tok_blk = lambda w: pl.BlockSpec((None, PROJ_TILE, w), lambda b, t: (b, t, 0))
    feat_blk = lambda r: pl.BlockSpec((None, r, PROJ_TILE), lambda b, t: (b, 0, t))
    vt_blk = pl.BlockSpec((None, kv_per_tile, 512, KV_TILE), lambda b, t: (b, t, 0, 0))
    full = lambda a: pl.BlockSpec(a.shape, lambda b, t: (0,) * a.ndim)
    tab_tok = pl.BlockSpec((PROJ_TILE, LANES), lambda b, t: (t, 0))
    tab_feat = lambda a: pl.BlockSpec((a.shape[0], PROJ_TILE), lambda b, t: (0, t))
    tok_tabs, feat_tabs = tables[:6], tables[6:]
    sds = jax.ShapeDtypeStruct
    qat, dq1t, dq2t, vat, dvt, ka, dk, ga, gb = pl.pallas_call(
        _proj_kernel,
        grid=(batch, seq_tiles),
        in_specs=[tok_blk(d_model), full(g2), full(w_bf), full(kvg2),
                  full(wuk), full(wuvt)] + [tab_tok] * 6 + [tab_feat(a) for a in feat_tabs],
        out_specs=[feat_blk(1024), feat_blk(512), feat_blk(512), vt_blk, vt_blk,
                   tok_blk(1024), tok_blk(512), tok_blk(512), tok_blk(512)],
        out_shape=[sds((batch, 1024, seq), BF16), sds((batch, 512, seq), BF16),
                   sds((batch, 512, seq), BF16),
                   sds((batch, n_kv, 512, KV_TILE), BF16), sds((batch, n_kv, 512, KV_TILE), BF16),
                   sds((batch, seq, 1024), BF16), sds((batch, seq, 512), BF16),
                   sds((batch, seq, 512), BF16), sds((batch, seq, 512), BF16)],
        compiler_params=pltpu.CompilerParams(
            dimension_semantics=("arbitrary", "arbitrary"), vmem_limit_bytes=VMEM_LIMIT_BYTES),
        name="proj",
    )(x3, g2, w_bf, kvg2, wuk, wuvt, *tok_tabs, *feat_tabs)

    q_tok = lambda w: pl.BlockSpec((None, Q_TILE, w), lambda b, i: (b, i, 0))
    q_feat = lambda r: pl.BlockSpec((None, r, Q_TILE), lambda b, i: (b, 0, i))
    kv_tok = lambda w: pl.BlockSpec((None, seq, w), lambda b, i: (b, 0, 0))
    kv_feat = pl.BlockSpec((None, n_kv, 512, KV_TILE), lambda b, i: (b, 0, 0, 0))
    fullb = lambda a: pl.BlockSpec(a.shape, lambda b, i: (0,) * a.ndim)
    wout = w_out.astype(BF16)
    lvecs = [v.reshape(1, DIFF_QK) for v in (lq1, lk1, lq2, lk2)]
    subg2 = subln_g.reshape(1, DIFF_V)
    postg2 = ln_post_g.reshape(1, d_model)
    return pl.pallas_call(
        functools.partial(_attn_kernel, lambda_init=lambda_init),
        grid=(batch, seq // Q_TILE),
        in_specs=[q_feat(1024), q_feat(512), q_feat(512), q_tok(512), q_tok(512), q_tok(d_model),
                  kv_tok(1024), kv_feat, kv_tok(512), kv_feat, fullb(wout)]
                 + [fullb(v) for v in lvecs] + [fullb(subg2), fullb(postg2)],
        out_specs=q_tok(d_model),
        out_shape=sds((batch, seq, d_model), F32),
        scratch_shapes=[
            pltpu.VMEM((N_MAPS, 1, Q_TILE), F32),
            pltpu.VMEM((N_MAPS, ACC_ROWS, Q_TILE), F32),
            pltpu.VMEM((SCORE_LOOKAHEAD, KV_TILE, Q_TILE), F32),
        ],
        compiler_params=pltpu.CompilerParams(
            dimension_semantics=("arbitrary", "arbitrary"), vmem_limit_bytes=VMEM_LIMIT_BYTES),
        name="attn",
    )(qat, dq1t, dq2t, ga, gb, x3, ka, vat, dk, dvt, wout, *lvecs, subg2, postg2)


def kernel(x, ln_pre_g, w_in, kv_norm_g, w_uk, w_uv, lambda_q1, lambda_k1,
           lambda_q2, lambda_k2, subln_g, w_out, ln_post_g):
    batch, seq, d_model = x.shape
    depth = w_in.shape[0]
    assert seq % PROJ_TILE == 0 and PROJ_TILE % KV_TILE == 0 and seq % Q_TILE == 0
    assert w_in.shape[2] == _C_END and d_model == 1024
    tables = _rope_tables(seq)
    for l in range(depth):
        lambda_init = 0.8 - 0.6 * math.exp(-0.3 * l)
        x = _layer(x, ln_pre_g[l], w_in[l], kv_norm_g[l], w_uk[l], w_uv[l],
                   lambda_q1[l], lambda_k1[l], lambda_q2[l], lambda_k2[l], subln_g[l],
                   w_out[l], ln_post_g[l], tables, lambda_init)
    return x
```

```python
import functools
import math

import jax
import jax.numpy as jnp
import numpy as np
from jax import lax
from jax.experimental import pallas as pl
from jax.experimental.pallas import tpu as pltpu

ROPE_THETA = 500000.0
NORM_EPS = 1e-6
SUBLN_EPS = 1e-5

MLA_HEADS = 4
MLA_NOPE = 128
MLA_ROPE = 64
MLA_V = 128
KV_RANK = 128
DIFF_HEADS = 4
DIFF_QK = 64
DIFF_V = 2 * DIFF_QK
DIFF_ROPE = DIFF_QK // 4

LANES = 128
BF16_ROWS = 16
MLA_QK_PAD = 2 * LANES
LOG2E = math.log2(math.e)
MLA_QSCALE = LOG2E / math.sqrt(MLA_NOPE + MLA_ROPE)
DIFF_QSCALE = LOG2E / math.sqrt(DIFF_QK)

PROJ_TILE = 512
KV_TILE = 256
Q_TILE = 2 * KV_TILE
N_MAPS = MLA_HEADS + 2 * DIFF_HEADS
ACC_ROWS = MLA_V + BF16_ROWS
SCORE_LOOKAHEAD = 2
VMEM_LIMIT_BYTES = 56 * 1024 * 1024

BF16 = jnp.bfloat16
F32 = jnp.float32
_NT = (((1,), (1,)), ((), ()))

_C_QN = 0
_C_QR = _C_QN + MLA_HEADS * MLA_NOPE
_C_CKV = _C_QR + MLA_HEADS * MLA_ROPE
_C_KR = _C_CKV + KV_RANK
_C_GA = _C_KR + MLA_ROPE
_C_DQ = _C_GA + MLA_HEADS * MLA_V
_C_DK = _C_DQ + DIFF_HEADS * 2 * DIFF_QK
_C_DV = _C_DK + DIFF_HEADS * 2 * DIFF_QK
_C_GB = _C_DV + DIFF_HEADS * DIFF_V
_C_END = _C_GB + DIFF_HEADS * DIFF_V
PROJ_COLS = -(-_C_END // LANES) * LANES
PROJ_BLOCK_RUNS = ((6, 12), (12, 20), (20, 28), (0, 6))
WEIGHT_CAST_ROWS = 128


def _rope_tables(seq):
    def tab(dim):
        inv_freq = ROPE_THETA ** (-np.arange(0, dim, 2, dtype=np.float64) / dim)
        ang = np.arange(seq, dtype=np.float64)[:, None] * inv_freq[None, :]
        return np.cos(ang), np.sin(ang)

    def lane_tables(c, s, group, rest_cos):
        half = c.shape[1]
        rest = np.full((seq, group - 2 * half), rest_cos)
        zero = np.zeros((seq, group - 2 * half))
        zh = np.zeros((seq, half))
        reps = LANES // group
        return (np.concatenate([c, c, rest] * reps, axis=1),
                np.concatenate([-s, zh, zero] * reps, axis=1),
                np.concatenate([zh, s, zero] * reps, axis=1))

    ca, sa = tab(MLA_ROPE)
    cb, sb = tab(DIFF_ROPE)
    tables = lane_tables(ca, sa, LANES, 0.0)
    tables += lane_tables(cb, sb, DIFF_QK, 1.0)
    tables += (ca.T, sa.T, cb.T, sb.T)
    return tuple(jnp.asarray(t, F32) for t in tables)


def _silu(g):
    return g / (1.0 + jnp.exp(-g))


def _rope_lanes(v, cos, sin_up, sin_down, half):
    return (v * cos + pltpu.roll(v, LANES - half, axis=1) * sin_up
            + pltpu.roll(v, half, axis=1) * sin_down)


def _proj_kernel(x_ref, g_ref, w_ref, kvg_ref, wuk_ref, wuvt_ref,
                 ca_ref, sau_ref, sad_ref, cb_ref, sbu_ref, sbd_ref,
                 cat_ref, sat_ref, cbt_ref, sbt_ref,
                 qat_ref, dq1t_ref, dq2t_ref, vat_ref, dvt_ref, ka_ref, dk_ref, ga_ref, gb_ref,
                 wbf_ref):
    @pl.when((pl.program_id(0) == 0) & (pl.program_id(1) == 0))
    def _():
        whole = _C_END // LANES * LANES
        for r in range(0, w_ref.shape[0], WEIGHT_CAST_ROWS):
            rows = slice(r, r + WEIGHT_CAST_ROWS)
            wbf_ref[rows, :whole] = w_ref[rows, :whole].astype(BF16)
            tail = w_ref[rows, whole:].astype(BF16)
            wbf_ref[rows, whole:] = jnp.concatenate(
                [tail, jnp.zeros((WEIGHT_CAST_ROWS, PROJ_COLS - _C_END), BF16)], axis=1)

    x = x_ref[...]
    ms = jnp.mean(x * x, axis=-1, keepdims=True)
    h = (x * lax.rsqrt(ms + NORM_EPS) * g_ref[...]).astype(BF16)

    blocks = {}
    for lo_blk, hi_blk in PROJ_BLOCK_RUNS:
        run = jnp.dot(h, wbf_ref[:, lo_blk * LANES:hi_blk * LANES], preferred_element_type=F32)
        for b in range(lo_blk, hi_blk):
            blocks[b] = run[:, (b - lo_blk) * LANES:(b - lo_blk + 1) * LANES]
    half_lanes = lax.broadcasted_iota(jnp.int32, (1, LANES), 1) < LANES // 2

    def tok_cols(col, width):
        first = col // LANES
        if col % LANES == 0:
            return jnp.concatenate([blocks[first + j] for j in range(width // LANES)], axis=1)
        assert col % LANES == LANES // 2 and width % LANES == 0
        swapped = [pltpu.roll(blocks[first + j], LANES // 2, axis=1)
                   for j in range(width // LANES + 1)]
        return jnp.concatenate([jnp.where(half_lanes, swapped[j], swapped[j + 1])
                                for j in range(width // LANES)], axis=1)

    def feat_rows(col, width):
        lo = col // LANES
        hi = -(-(col + width) // LANES)
        rows = jnp.concatenate([blocks[b].T for b in range(lo, hi)], axis=0)
        return rows[col - lo * LANES:col - lo * LANES + width]

    feat_q = feat_rows(_C_QN, _C_CKV - _C_QN)
    feat_dq = feat_rows(_C_DQ, _C_DK - _C_DQ)
    feat_dv = feat_rows(_C_DV, _C_GB - _C_DV)

    cat, sat = cat_ref[...], sat_ref[...]
    half = MLA_ROPE // 2
    zpad = jnp.zeros((MLA_QK_PAD - MLA_NOPE - MLA_ROPE, PROJ_TILE), BF16)
    for hd in range(MLA_HEADS):
        base = hd * MLA_QK_PAD
        qn = feat_q[hd * MLA_NOPE:(hd + 1) * MLA_NOPE]
        qat_ref[base:base + MLA_NOPE, :] = (qn * MLA_QSCALE).astype(BF16)
        qr = feat_q[_C_QR + hd * MLA_ROPE:_C_QR + (hd + 1) * MLA_ROPE]
        x1, x2 = qr[:half], qr[half:]
        r1 = (x1 * cat - x2 * sat) * MLA_QSCALE
        r2 = (x2 * cat + x1 * sat) * MLA_QSCALE
        rope_rows = jnp.concatenate([r1.astype(BF16), r2.astype(BF16), zpad], axis=0)
        qat_ref[base + MLA_NOPE:base + MLA_QK_PAD, :] = rope_rows

    cbt, sbt = cbt_ref[...], sbt_ref[...]
    rh = DIFF_ROPE // 2
    zmap = jnp.zeros((DIFF_QK, PROJ_TILE), BF16)
    for hd in range(DIFF_HEADS):
        maps = []
        for mp in range(2):
            d = feat_dq[(2 * hd + mp) * DIFF_QK:(2 * hd + mp + 1) * DIFF_QK]
            x1, x2 = d[:rh], d[rh:2 * rh]
            q = jnp.concatenate([x1 * cbt - x2 * sbt, x2 * cbt + x1 * sbt, d[2 * rh:]], axis=0)
            maps.append((q * DIFF_QSCALE).astype(BF16))
        sl = slice(hd * LANES, (hd + 1) * LANES)
        dq1t_ref[sl, :] = jnp.concatenate([maps[0], zmap], axis=0)
        dq2t_ref[sl, :] = jnp.concatenate([zmap, maps[1]], axis=0)
    for c in range(PROJ_TILE // KV_TILE):
        dvt_ref[c] = feat_dv[:, c * KV_TILE:(c + 1) * KV_TILE].astype(BF16)

    c = tok_cols(_C_CKV, KV_RANK)
    c = c * lax.rsqrt(jnp.mean(c * c, axis=-1, keepdims=True) + NORM_EPS) * kvg_ref[...]
    c = c.astype(BF16)
    kn = jnp.dot(c, wuk_ref[...], preferred_element_type=F32)
    vt = lax.dot_general(wuvt_ref[...], c, _NT, preferred_element_type=F32)
    for ci in range(PROJ_TILE // KV_TILE):
        vat_ref[ci] = vt[:, ci * KV_TILE:(ci + 1) * KV_TILE].astype(BF16)
    kr = _rope_lanes(tok_cols(_C_KR, LANES), ca_ref[...], sau_ref[...], sad_ref[...], half)
    kr = kr.astype(BF16)
    for hd in range(MLA_HEADS):
        base = hd * MLA_QK_PAD
        ka_ref[:, base:base + LANES] = kn[:, hd * LANES:(hd + 1) * LANES].astype(BF16)
        ka_ref[:, base + LANES:base + 2 * LANES] = kr
    ga_ref[...] = _silu(tok_cols(_C_GA, _C_DQ - _C_GA)).astype(BF16)
    cb, sbu, sbd = cb_ref[...], sbu_ref[...], sbd_ref[...]
    dk_tok = tok_cols(_C_DK, _C_DV - _C_DK)
    for hd in range(DIFF_HEADS):
        sl = slice(hd * LANES, (hd + 1) * LANES)
        dk_ref[:, sl] = _rope_lanes(dk_tok[:, sl], cb, sbu, sbd, rh).astype(BF16)
    gb_ref[...] = _silu(tok_cols(_C_GB, _C_END - _C_GB)).astype(BF16)


def _softmax_pv(s_t, vt, m_ref, acc_ref, idx, lanes):
    m_old = m_ref[idx, :, lanes]
    m_new = jnp.maximum(m_old, jnp.max(s_t, axis=0, keepdims=True))
    alpha = jnp.exp2(m_old - m_new)
    p_t = jnp.exp2(s_t - m_new).astype(BF16)
    pv = jnp.dot(vt, p_t, preferred_element_type=F32)
    acc_ref[idx, :, lanes] = alpha * acc_ref[idx, :, lanes] + pv
    m_ref[idx, :, lanes] = m_new


def _run_pipelined(units, ready, following):
    pending = list(ready)
    stream = [scores for scores, _ in units] + [scores for scores, _ in following]
    for n, (_, update) in enumerate(units):
        if n + SCORE_LOOKAHEAD < len(stream):
            pending.append(stream[n + SCORE_LOOKAHEAD]())
        update(pending.pop(0))
    return pending


def _attn_kernel(qat_ref, dq1t_ref, dq2t_ref, ga_ref, gb_ref, x_ref,
                 ka_ref, vat_ref, dk_ref, dvt_ref, wout_ref,
                 lq1_ref, lk1_ref, lq2_ref, lk2_ref, subg_ref, postg_ref,
                 out_ref, m_ref, acc_ref, carry_ref, *, lambda_init):
    qi = pl.program_id(1)
    m_ref[...] = jnp.full(m_ref.shape, -jnp.inf, F32)
    acc_ref[...] = jnp.zeros(acc_ref.shape, F32)

    ones = jnp.ones((BF16_ROWS, KV_TILE), BF16)
    key = lax.broadcasted_iota(jnp.int32, (KV_TILE, Q_TILE), 0)
    qry = lax.broadcasted_iota(jnp.int32, (KV_TILE, Q_TILE), 1)
    causal = key <= qry
    all_q = slice(0, Q_TILE)
    upper_q = slice(KV_TILE, Q_TILE)

    def tile_units(j, lanes, mask):
        rows = pl.ds(pl.multiple_of(j * KV_TILE, KV_TILE), KV_TILE)
        units = []
        for u in range(N_MAPS):
            if u < MLA_HEADS:
                cs = slice(u * MLA_QK_PAD, (u + 1) * MLA_QK_PAD)
                vs = slice(u * MLA_V, (u + 1) * MLA_V)
                k_ref, qt_ref, v_ref = ka_ref, qat_ref, vat_ref
            else:
                d = u - MLA_HEADS
                cs = vs = slice((d // 2) * LANES, (d // 2 + 1) * LANES)
                k_ref, qt_ref, v_ref = dk_ref, (dq2t_ref if d % 2 else dq1t_ref), dvt_ref

            def scores(cs=cs, k_ref=k_ref, qt_ref=qt_ref):
                return jnp.dot(k_ref[rows, cs], qt_ref[cs, lanes], preferred_element_type=F32)

            def update(s_t, u=u, vs=vs, v_ref=v_ref):
                if mask is not None:
                    s_t = jnp.where(mask, s_t, -jnp.inf)
                vt = jnp.concatenate([v_ref[j, vs, :], ones], axis=0)
                _softmax_pv(s_t, vt, m_ref, acc_ref, u, lanes)

            units.append((scores, update))
        return units

    def load_carry():
        return [carry_ref[n] for n in range(SCORE_LOOKAHEAD)]

    def store_carry(scores):
        for n, s_t in enumerate(scores):
            carry_ref[n] = s_t

    store_carry([scores() for scores, _ in tile_units(0, all_q, None)[:SCORE_LOOKAHEAD]])

    def body(i, carry):
        units = tile_units(2 * i, all_q, None) + tile_units(2 * i + 1, all_q, None)
        store_carry(_run_pipelined(units, load_carry(), tile_units(2 * i + 2, all_q, None)))
        return carry

    lax.fori_loop(0, qi, body, 0)
    _run_pipelined(tile_units(2 * qi, all_q, causal)
                   + tile_units(2 * qi + 1, upper_q, causal[:, :KV_TILE]), load_carry(), [])

    lam = (jnp.exp(jnp.sum(lq1_ref[...] * lk1_ref[...], axis=1, keepdims=True))
           - jnp.exp(jnp.sum(lq2_ref[...] * lk2_ref[...], axis=1, keepdims=True))
           + lambda_init)

    def normalised(u):
        acc = acc_ref[u]
        return acc[:MLA_V] * (1.0 / acc[MLA_V:MLA_V + 1])

    oa = jnp.concatenate([normalised(h).T for h in range(MLA_HEADS)], axis=1)
    oa = oa * ga_ref[...].astype(F32)
    ob = []
    for h in range(DIFF_HEADS):
        o_t = normalised(MLA_HEADS + 2 * h) - lam * normalised(MLA_HEADS + 2 * h + 1)
        o_t = o_t * lax.rsqrt(jnp.mean(o_t * o_t, axis=0, keepdims=True) + SUBLN_EPS)
        ob.append(o_t.T * subg_ref[...] * (1.0 - lambda_init))
    ob = jnp.concatenate(ob, axis=1) * gb_ref[...].astype(F32)

    mixed_in = jnp.concatenate([oa, ob], axis=1).astype(BF16)
    mixed = jnp.dot(mixed_in, wout_ref[...], preferred_element_type=F32)
    y = mixed * lax.rsqrt(jnp.mean(mixed * mixed, axis=-1, keepdims=True) + NORM_EPS)
    out_ref[...] = x_ref[...] + y * postg_ref[...]


def _layer(x3, ln_pre_g, w_in, kv_norm_g, w_uk, w_uv, lq1, lk1, lq2, lk2,
           subln_g, w_out, ln_post_g, tables, lambda_init):
    batch, seq, d_model = x3.shape
    g2 = ln_pre_g.reshape(1, d_model)
    kvg2 = kv_norm_g.reshape(1, KV_RANK)
    wuk = w_uk.astype(BF16)
    wuvt = w_uv.T.astype(BF16)

    seq_tiles = seq // PROJ_TILE
    kv_per_tile = PROJ_TILE // KV_TILE
    n_kv = seq // KV_TILE
    tok_blk = lambda w: pl.BlockSpec((None, PROJ_TILE, w), lambda b, t: (b, t, 0))
    feat_blk = lambda r: pl.BlockSpec((None, r, PROJ_TILE), lambda b, t: (b, 0, t))
    vt_blk = pl.BlockSpec((None, kv_per_tile, 512, KV_TILE), lambda b, t: (b, t, 0, 0))
    full = lambda a: pl.BlockSpec(a.shape, lambda b, t: (0,) * a.ndim)
    tab_tok = pl.BlockSpec((PROJ_TILE, LANES), lambda b, t: (t, 0))
    tab_feat = lambda a: pl.BlockSpec((a.shape[0], PROJ_TILE), lambda b, t: (0, t))
    tok_tabs, feat_tabs = tables[:6], tables[6:]
    sds = jax.ShapeDtypeStruct
    qat, dq1t, dq2t, vat, dvt, ka, dk, ga, gb = pl.pallas_call(
        _proj_kernel,
        grid=(batch, seq_tiles),
        in_specs=[tok_blk(d_model), full(g2),
                  pl.BlockSpec(w_in.shape, lambda b, t: (0, 0), pipeline_mode=pl.Buffered(1)),
                  full(kvg2), full(wuk), full(wuvt)]
                 + [tab_tok] * 6 + [tab_feat(a) for a in feat_tabs],
        out_specs=[feat_blk(1024), feat_blk(512), feat_blk(512), vt_blk, vt_blk,
                   tok_blk(1024), tok_blk(512), tok_blk(512), tok_blk(512)],
        out_shape=[sds((batch, 1024, seq), BF16), sds((batch, 512, seq), BF16),
                   sds((batch, 512, seq), BF16),
                   sds((batch, n_kv, 512, KV_TILE), BF16), sds((batch, n_kv, 512, KV_TILE), BF16),
                   sds((batch, seq, 1024), BF16), sds((batch, seq, 512), BF16),
                   sds((batch, seq, 512), BF16), sds((batch, seq, 512), BF16)],
        scratch_shapes=[pltpu.VMEM((d_model, PROJ_COLS), BF16)],
        compiler_params=pltpu.CompilerParams(
            dimension_semantics=("arbitrary", "arbitrary"), vmem_limit_bytes=VMEM_LIMIT_BYTES),
        name="proj",
    )(x3, g2, w_in, kvg2, wuk, wuvt, *tok_tabs, *feat_tabs)

    q_tok = lambda w: pl.BlockSpec((None, Q_TILE, w), lambda b, i: (b, i, 0))
    q_feat = lambda r: pl.BlockSpec((None, r, Q_TILE), lambda b, i: (b, 0, i))
    kv_tok = lambda w: pl.BlockSpec((None, seq, w), lambda b, i: (b, 0, 0))
    kv_feat = pl.BlockSpec((None, n_kv, 512, KV_TILE), lambda b, i: (b, 0, 0, 0))
    fullb = lambda a: pl.BlockSpec(a.shape, lambda b, i: (0,) * a.ndim)
    wout = w_out.astype(BF16)
    lvecs = [v.reshape(1, DIFF_QK) for v in (lq1, lk1, lq2, lk2)]
    subg2 = subln_g.reshape(1, DIFF_V)
    postg2 = ln_post_g.reshape(1, d_model)
    return pl.pallas_call(
        functools.partial(_attn_kernel, lambda_init=lambda_init),
        grid=(batch, seq // Q_TILE),
        in_specs=[q_feat(1024), q_feat(512), q_feat(512), q_tok(512), q_tok(512), q_tok(d_model),
                  kv_tok(1024), kv_feat, kv_tok(512), kv_feat, fullb(wout)]
                 + [fullb(v) for v in lvecs] + [fullb(subg2), fullb(postg2)],
        out_specs=q_tok(d_model),
        out_shape=sds((batch, seq, d_model), F32),
        scratch_shapes=[
            pltpu.VMEM((N_MAPS, 1, Q_TILE), F32),
            pltpu.VMEM((N_MAPS, ACC_ROWS, Q_TILE), F32),
            pltpu.VMEM((SCORE_LOOKAHEAD, KV_TILE, Q_TILE), F32),
        ],
        compiler_params=pltpu.CompilerParams(
            dimension_semantics=("arbitrary", "arbitrary"), vmem_limit_bytes=VMEM_LIMIT_BYTES),
        name="attn",
    )(qat, dq1t, dq2t, ga, gb, x3, ka, vat, dk, dvt, wout, *lvecs, subg2, postg2)


def kernel(x, ln_pre_g, w_in, kv_norm_g, w_uk, w_uv, lambda_q1, lambda_k1,
           lambda_q2, lambda_k2, subln_g, w_out, ln_post_g):
    batch, seq, d_model = x.shape
    depth = w_in.shape[0]
    assert seq % PROJ_TILE == 0 and PROJ_TILE % KV_TILE == 0 and seq % Q_TILE == 0
    assert w_in.shape[2] == _C_END and d_model == 1024
    tables = _rope_tables(seq)
    for l in range(depth):
        lambda_init = 0.8 - 0.6 * math.exp(-0.3 * l)
        x = _layer(x, ln_pre_g[l], w_in[l], kv_norm_g[l], w_uk[l], w_uv[l],
                   lambda_q1[l], lambda_k1[l], lambda_q2[l], lambda_k2[l], subln_g[l],
                   w_out[l], ln_post_g[l], tables, lambda_init)
    return x
```
